```python
import math
import jax, jax.numpy as jnp
from jax import lax
import numpy as np

D_MODEL = 1024
BATCH = 2
SEQ = 8192
DEPTH = 1
DEC_BATCH = 128
DEC_SEQ = 4
PAST_LEN = 8192
PAGE_SIZE = 128

ATT_HEAD_DIM = 64
ATT_HEADS = (D_MODEL // 2) // ATT_HEAD_DIM
ATT_KV_HEADS = ATT_HEADS // 4
ATT_GROUP = ATT_HEADS // ATT_KV_HEADS
ATT_WIDTH = ATT_HEADS * ATT_HEAD_DIM
WINDOW = 128
NUM_BUCKETS = 32
MAX_DISTANCE = 128
HG_KEY = 128
HG_VAL = 128
HG_HEADS = (D_MODEL // 2) // HG_VAL
HG_WIDTH = HG_HEADS * HG_VAL
HG_FDIM = HG_HEADS * HG_KEY
HG_CHUNK = 64
MIX_WIDTH = ATT_WIDTH + HG_WIDTH
D_FF = ((8 * D_MODEL // 3 + 127) // 128) * 128
CONV_W = 3
RMS_EPS = 1e-6
SPLITS = (ATT_WIDTH, ATT_KV_HEADS * ATT_HEAD_DIM, ATT_KV_HEADS * ATT_HEAD_DIM,
          HG_FDIM, HG_FDIM, HG_WIDTH, HG_WIDTH)
PROJ_WIDTH = sum(SPLITS)
SPLIT_IDX = [sum(SPLITS[:i + 1]) for i in range(len(SPLITS) - 1)]

kernel_name = 'hymba_swa_sink_hgrn2_convffn_step'


def rms_norm(x, g):
    xf = x.astype(jnp.float32)
    y = xf * lax.rsqrt(jnp.mean(xf * xf, axis=-1, keepdims=True) + RMS_EPS)
    return (y * g.astype(jnp.float32)).astype(x.dtype)


def t5_bucket(dist):
    n = jnp.maximum(dist, 0)
    max_exact = NUM_BUCKETS // 2
    nf = jnp.maximum(n, 1).astype(jnp.float32)
    large = max_exact + (jnp.log(nf / max_exact) / math.log(MAX_DISTANCE / max_exact)
                         * (NUM_BUCKETS - max_exact)).astype(jnp.int32)
    large = jnp.minimum(large, NUM_BUCKETS - 1)
    return jnp.where(n < max_exact, n, large)


def sink_window_attention(q, k, v, dist, valid, rel_bias, sinks):
    nq, nk = dist.shape
    s = jnp.einsum('bnqhgd,bnshd->bnhgqs', q.astype(jnp.float32), k.astype(jnp.float32)) * (ATT_HEAD_DIM ** -0.5)
    bias = rel_bias.astype(jnp.float32)[t5_bucket(dist)]
    bias = bias.reshape(nq, nk, ATT_KV_HEADS, ATT_GROUP).transpose(2, 3, 0, 1)
    mask = valid[:, None, None] & ((dist >= 0) & (dist < WINDOW))
    s = jnp.where(mask, s + bias, -jnp.inf)
    sink = sinks.astype(jnp.float32).reshape(ATT_KV_HEADS, ATT_GROUP)[:, :, None]
    m = jnp.maximum(jnp.max(s, axis=-1), sink)
    p = jnp.exp(s - m[..., None])
    den = jnp.sum(p, axis=-1) + jnp.exp(sink - m)
    o = jnp.einsum('bnhgqs,bnshd->bnqhgd', p, v.astype(jnp.float32))
    return o / den.transpose(0, 1, 4, 2, 3)[..., None]


def attn_prompt(q, k, v, rel_bias, sinks):
    b, s_len = q.shape[:2]
    nb = s_len // WINDOW
    qb = q.reshape(b, nb, WINDOW, ATT_KV_HEADS, ATT_GROUP, ATT_HEAD_DIM)

    def band(t):
        tp = jnp.pad(t, ((0, 0), (WINDOW, 0), (0, 0), (0, 0)))
        tp = tp.reshape(b, nb + 1, WINDOW, ATT_KV_HEADS, ATT_HEAD_DIM)
        return jnp.concatenate([tp[:, :-1], tp[:, 1:]], axis=2)

    qi = jnp.arange(WINDOW)[:, None]
    si = jnp.arange(2 * WINDOW)[None, :]
    dist = WINDOW + qi - si
    valid = (jnp.arange(nb)[:, None, None] * WINDOW - WINDOW + si[None]) >= 0
    o = sink_window_attention(qb, band(k), band(v), dist, valid, rel_bias, sinks)
    return o.reshape(b, s_len, ATT_WIDTH)


def attn_sample(q, k, v, k_buf, v_buf, rel_bias, sinks):
    b, l = q.shape[:2]
    wb = k_buf.shape[1]
    kc = jnp.concatenate([k_buf.astype(k.dtype), k], axis=1)
    vc = jnp.concatenate([v_buf.astype(v.dtype), v], axis=1)
    dist = jnp.arange(l)[:, None] + wb - jnp.arange(wb + l)[None, :]
    valid = jnp.ones((1, 1, wb + l), dtype=bool)
    qb = q.reshape(b, 1, l, ATT_KV_HEADS, ATT_GROUP, ATT_HEAD_DIM)
    o = sink_window_attention(qb, kc[:, None], vc[:, None], dist, valid, rel_bias, sinks)
    return o.reshape(b, l, ATT_WIDTH), kc[:, -wb:], vc[:, -wb:]


def hgrn2_recurrence(q, logf, k, v, s0):
    b, l, h, dk = q.shape
    c = HG_CHUNK if l % HG_CHUNK == 0 else l
    nc = l // c

    def to_chunks(t):
        return t.astype(jnp.float32).reshape(b, nc, c, h, t.shape[-1]).transpose(1, 0, 3, 2, 4)

    causal = jnp.tril(jnp.ones((c, c), dtype=bool))

    def step(s, xs):
        qc, lfc, kc, vc = xs
        cb = jnp.cumsum(lfc, axis=2)
        o = jnp.einsum('bhtk,bhkv->bhtv', qc * jnp.exp(cb), s)
        decay = jnp.exp(jnp.where(causal[:, :, None], cb[:, :, :, None, :] - cb[:, :, None, :, :], -jnp.inf))
        a = jnp.einsum('bhtk,bhtsk,bhsk->bhts', qc, decay, kc)
        o = o + jnp.einsum('bhts,bhsv->bhtv', a, vc)
        bl = cb[:, :, -1:, :]
        s = jnp.exp(bl[:, :, 0])[..., None] * s + jnp.einsum('bhsk,bhsv->bhkv', kc * jnp.exp(bl - cb), vc)
        return s, o

    s_fin, o = lax.scan(step, s0.astype(jnp.float32), (to_chunks(q), to_chunks(logf), to_chunks(k), to_chunks(v)))
    o = o.transpose(1, 0, 3, 2, 4).reshape(b, l, h, v.shape[-1])
    return o, s_fin


def mixer(h, w_in, w_out, sinks, attn_g, hg_g, lb, rel_bias, k_buf, v_buf, s0):
    b, l = h.shape[:2]
    qa, ka, va, qh, fh, ih, gh = jnp.split(h @ w_in, SPLIT_IDX, axis=-1)
    qa = qa.reshape(b, l, ATT_HEADS, ATT_HEAD_DIM)
    ka = ka.reshape(b, l, ATT_KV_HEADS, ATT_HEAD_DIM)
    va = va.reshape(b, l, ATT_KV_HEADS, ATT_HEAD_DIM)
    if k_buf is None:
        o_a = attn_prompt(qa, ka, va, rel_bias, sinks)
        wb = min(WINDOW, l)
        k_new, v_new = ka[:, -wb:], va[:, -wb:]
        s0 = jnp.zeros((b, HG_HEADS, HG_KEY, HG_VAL), jnp.float32)
    else:
        o_a, k_new, v_new = attn_sample(qa, ka, va, k_buf, v_buf, rel_bias, sinks)
    q_h = jax.nn.silu(qh.astype(jnp.float32)).reshape(b, l, HG_HEADS, HG_KEY)
    f = lb + (1.0 - lb) * jax.nn.sigmoid(fh.astype(jnp.float32))
    logf = jnp.log(f).reshape(b, l, HG_HEADS, HG_KEY)
    k_h = (1.0 - f).reshape(b, l, HG_HEADS, HG_KEY)
    v_h = ih.astype(jnp.float32).reshape(b, l, HG_HEADS, HG_VAL)
    o_h, s_new = hgrn2_recurrence(q_h, logf, k_h, v_h, s0)
    o_h = o_h * lax.rsqrt(jnp.mean(o_h * o_h, axis=-1, keepdims=True) + RMS_EPS)
    o_h = o_h.reshape(b, l, HG_WIDTH) * hg_g.astype(jnp.float32) * jax.nn.silu(gh.astype(jnp.float32))
    o_a = rms_norm(o_a, attn_g)
    out = jnp.concatenate([o_a, o_h], axis=-1).astype(h.dtype) @ w_out
    return out, k_new, v_new, s_new


def conv_ffn(h, w_in, conv_w, conv_b, w_out, prev):
    b, l = h.shape[:2]
    a, g = jnp.split(h @ w_in, 2, axis=-1)
    if prev is None:
        prev = jnp.zeros((b, CONV_W - 1, D_FF), a.dtype)
    ap = jnp.concatenate([prev.astype(a.dtype), a], axis=1)
    ac = conv_b + ap[:, 0:l] * conv_w[0]
    for j in range(1, CONV_W):
        ac = ac + ap[:, j:j + l] * conv_w[j]
    y = (jax.nn.silu(ac) * g) @ w_out
    return y, ap[:, -(CONV_W - 1):]


def layer(x, lp, lb, rel_bias, k_buf, v_buf, s0, conv_prev):
    n1, w_in, sinks, attn_g, hg_g, w_o, n2, wf_in, cw, cb, wf_out = lp
    mix, k_new, v_new, s_new = mixer(rms_norm(x, n1), w_in, w_o, sinks, attn_g, hg_g, lb, rel_bias, k_buf, v_buf, s0)
    x = x + mix
    ff, c_new = conv_ffn(rms_norm(x, n2), wf_in, cw, cb, wf_out, conv_prev)
    x = x + ff
    return x, k_new, v_new, s_new, c_new


def setup_inputs(seed: int = 0) -> dict:
    key = jax.random.key(seed)
    ks = jax.random.split(key, 24)
    f32 = jnp.float32

    def nrm(k, shape, scale):
        return jax.random.normal(k, shape, f32) * scale

    win = min(WINDOW, PAST_LEN)
    return {
        'x_prompt': nrm(ks[0], (BATCH, SEQ, D_MODEL), 1.0),
        'x_sample': nrm(ks[1], (DEC_BATCH, DEC_SEQ, D_MODEL), 1.0),
        'cache_k_win': nrm(ks[2], (DEPTH, DEC_BATCH, win, ATT_KV_HEADS, ATT_HEAD_DIM), 1.0),
        'cache_v_win': nrm(ks[3], (DEPTH, DEC_BATCH, win, ATT_KV_HEADS, ATT_HEAD_DIM), 1.0),
        'state_hgrn': nrm(ks[4], (DEPTH, DEC_BATCH, HG_HEADS, HG_KEY, HG_VAL), 0.3),
        'state_conv': nrm(ks[5], (DEPTH, DEC_BATCH, CONV_W - 1, D_FF), 1.0),
        'norm1_g': 1.0 + nrm(ks[6], (DEPTH, D_MODEL), 0.01),
        'w_in': nrm(ks[7], (DEPTH, D_MODEL, PROJ_WIDTH), D_MODEL ** -0.5),
        'attn_sinks': nrm(ks[8], (DEPTH, ATT_HEADS), 0.5),
        'rel_bias': nrm(ks[9], (NUM_BUCKETS, ATT_HEADS), 0.1),
        'lb_gamma': nrm(ks[10], (DEPTH + 1, HG_FDIM), 0.1),
        'attn_out_g': 1.0 + nrm(ks[11], (DEPTH, ATT_WIDTH), 0.01),
        'hg_out_g': 1.0 + nrm(ks[12], (DEPTH, HG_WIDTH), 0.01),
        'w_out': nrm(ks[13], (DEPTH, MIX_WIDTH, D_MODEL), MIX_WIDTH ** -0.5),
        'norm2_g': 1.0 + nrm(ks[14], (DEPTH, D_MODEL), 0.01),
        'w_ffn_in': nrm(ks[15], (DEPTH, D_MODEL, 2 * D_FF), D_MODEL ** -0.5),
        'conv_w': nrm(ks[16], (DEPTH, CONV_W, D_FF), CONV_W ** -0.5),
        'conv_b': nrm(ks[17], (DEPTH, D_FF), 0.01),
        'w_ffn_out': nrm(ks[18], (DEPTH, D_FF, D_MODEL), D_FF ** -0.5),
        'final_g': 1.0 + nrm(ks[19], (D_MODEL,), 0.01),
    }


def reference(x_prompt, x_sample, cache_k_win, cache_v_win, state_hgrn, state_conv,
              norm1_g, w_in, attn_sinks, rel_bias, lb_gamma, attn_out_g, hg_out_g, w_out,
              norm2_g, w_ffn_in, conv_w, conv_b, w_ffn_out, final_g):
    lb_all = jnp.cumsum(jax.nn.softmax(lb_gamma.astype(jnp.float32), axis=0), axis=0)
    xp, xs = x_prompt, x_sample
    kp_l, vp_l, sp_l, cp_l = [], [], [], []
    ks_l, vs_l, ss_l, cs_l = [], [], [], []
    for l in range(DEPTH):
        lp = (norm1_g[l], w_in[l], attn_sinks[l], attn_out_g[l], hg_out_g[l], w_out[l],
              norm2_g[l], w_ffn_in[l], conv_w[l], conv_b[l], w_ffn_out[l])
        xp, kp, vp, sp, cp = layer(xp, lp, lb_all[l], rel_bias, None, None, None, None)
        xs, ks_, vs_, ss_, cs_ = layer(xs, lp, lb_all[l], rel_bias, cache_k_win[l], cache_v_win[l],
                                       state_hgrn[l], state_conv[l])
        kp_l.append(kp); vp_l.append(vp); sp_l.append(sp); cp_l.append(cp)
        ks_l.append(ks_); vs_l.append(vs_); ss_l.append(ss_); cs_l.append(cs_)
    y_prompt = rms_norm(xp, final_g)
    y_sample = rms_norm(xs, final_g)
    return (y_prompt, y_sample,
            jnp.stack(kp_l), jnp.stack(vp_l), jnp.stack(sp_l), jnp.stack(cp_l),
            jnp.stack(ks_l), jnp.stack(vs_l), jnp.stack(ss_l), jnp.stack(cs_l))
```

```python
import functools
import math

import jax
import jax.numpy as jnp
import numpy as np
from jax import lax
from jax.experimental import pallas as pl
from jax.experimental.pallas import tpu as pltpu

D_MODEL = 1024
BATCH = 2
SEQ = 8192
DEPTH = 1
DEC_BATCH = 128
DEC_SEQ = 4
ATT_HEAD_DIM = 64
ATT_HEADS = 8
ATT_KV_HEADS = 2
ATT_GROUP = 4
ATT_WIDTH = 512
KV_WIDTH = ATT_KV_HEADS * ATT_HEAD_DIM
WINDOW = 128
NUM_BUCKETS = 32
MAX_DISTANCE = 128
HG_KEY = 128
HG_VAL = 128
HG_HEADS = 4
HG_WIDTH = 512
D_FF = 2816
CONV_W = 3
RMS_EPS = 1e-6
ATT_SLAB = ATT_WIDTH + 2 * KV_WIDTH
HG_SLAB = 4 * HG_WIDTH
PROJ_WIDTH = ATT_SLAB + HG_SLAB

V7X_SUBLANES = 8
V7X_LANES = 128
V7X_MXU_DIM = 256
V7X_VMEM_LIMIT_BYTES = 56 * 1024 * 1024

PROJ_ROWS = 512
FFN_ROWS = 512
FFN_COLS = V7X_MXU_DIM
HG_CHUNK = 128
SAMPLE_GROUP = 8
SAMPLE_KEYS = WINDOW + DEC_SEQ
SAMPLE_KEYS_PAD = SAMPLE_KEYS + (-SAMPLE_KEYS) % V7X_SUBLANES

_F32 = jnp.float32
_BF16 = jnp.bfloat16
_NEG_INF = float("-inf")


def _dot(a, b):
    return jnp.dot(a, b, preferred_element_type=_F32)


def _dot_nt(a, b):
    return lax.dot_general(a, b, (((1,), (1,)), ((), ())), preferred_element_type=_F32)


def _dot_tn(a, b):
    return lax.dot_general(a, b, (((0,), (0,)), ((), ())), preferred_element_type=_F32)


def _rms(x, g):
    return x * lax.rsqrt(jnp.mean(x * x, axis=-1, keepdims=True) + RMS_EPS) * g


def _t5_bucket_np(dist):
    dist = np.asarray(dist)
    n = np.maximum(dist, 0)
    max_exact = NUM_BUCKETS // 2
    nf = np.maximum(n, 1).astype(np.float64)
    large = max_exact + (np.log(nf / max_exact) / math.log(MAX_DISTANCE / max_exact)
                         * (NUM_BUCKETS - max_exact)).astype(np.int32)
    large = np.minimum(large, NUM_BUCKETS - 1)
    bucket = np.where(n < max_exact, n, large)
    return np.where((dist >= 0) & (dist < WINDOW), bucket, -1).astype(np.int32)


def _bias_from_buckets(bidx, relb_ref, head):
    def body(b, acc):
        return jnp.where(bidx == b, relb_ref[b * ATT_HEADS + head], acc)
    return lax.fori_loop(0, NUM_BUCKETS, body, jnp.full(bidx.shape, _NEG_INF, _F32))


def _proj_kernel(x_ref, g_ref, w_ref, att_ref, hg_ref):
    h = _rms(x_ref[...], g_ref[...]).astype(_BF16)
    res = _dot(h, w_ref[...])
    att_ref[...] = res[:, :ATT_SLAB]
    hg_ref[...] = res[:, ATT_SLAB:]


def _proj(x, g, w):
    t = x.shape[0]
    tm = PROJ_ROWS
    return pl.pallas_call(
        _proj_kernel,
        grid=(t // tm,),
        in_specs=[
            pl.BlockSpec((tm, D_MODEL), lambda i: (i, 0)),
            pl.BlockSpec((1, D_MODEL), lambda i: (0, 0)),
            pl.BlockSpec((D_MODEL, PROJ_WIDTH), lambda i: (0, 0)),
        ],
        out_specs=[
            pl.BlockSpec((tm, ATT_SLAB), lambda i: (i, 0)),
            pl.BlockSpec((tm, HG_SLAB), lambda i: (i, 0)),
        ],
        out_shape=[
            jax.ShapeDtypeStruct((t, ATT_SLAB), _F32),
            jax.ShapeDtypeStruct((t, HG_SLAB), _F32),
        ],
        compiler_params=pltpu.CompilerParams(
            dimension_semantics=("arbitrary",), vmem_limit_bytes=V7X_VMEM_LIMIT_BYTES),
        name="proj",
    )(x, g, w)


def _attn_prompt_kernel(relb_ref, sink_ref, bidx_ref, q_ref, kp_ref, kc_ref, vp_ref, vc_ref,
                        o_ref, bias_scr):
    b = pl.program_id(0)
    n = pl.program_id(1)

    @pl.when((b == 0) & (n == 0))
    def _():
        bidx = bidx_ref[...]
        for h in range(ATT_HEADS):
            bias_scr[h] = _bias_from_buckets(bidx, relb_ref, h)

    q = q_ref[...] * (ATT_HEAD_DIM ** -0.5)
    k = jnp.concatenate([kp_ref[...], kc_ref[...]], axis=0)
    v = jnp.concatenate([vp_ref[...], vc_ref[...]], axis=0)
    col = lax.broadcasted_iota(jnp.int32, (WINDOW, 2 * WINDOW), 1)
    no_prev = (col < WINDOW) & (n == 0)
    outs = []
    for kv in range(ATT_KV_HEADS):
        ds = slice(kv * ATT_HEAD_DIM, (kv + 1) * ATT_HEAD_DIM)
        kh = k[:, ds].astype(_BF16)
        vh = v[:, ds].astype(_BF16)
        for g in range(ATT_GROUP):
            h = kv * ATT_GROUP + g
            qh = q[:, h * ATT_HEAD_DIM:(h + 1) * ATT_HEAD_DIM].astype(_BF16)
            s = _dot_nt(qh, kh) + bias_scr[h]
            s = jnp.where(no_prev, _NEG_INF, s)
            sink = sink_ref[h]
            m = jnp.maximum(jnp.max(s, axis=-1, keepdims=True), sink)
            p = jnp.exp(s - m)
            den = jnp.sum(p, axis=-1, keepdims=True) + jnp.exp(sink - m)
            outs.append(_dot(p.astype(_BF16), vh) / den)
    o_ref[...] = jnp.concatenate(outs, axis=-1)


def _attn_prompt(att, rel_bias, sinks):
    bsz, l, _ = att.shape
    nb = l // WINDOW
    qi = np.arange(WINDOW)[:, None]
    si = np.arange(2 * WINDOW)[None, :]
    bidx = jnp.asarray(_t5_bucket_np(WINDOW + qi - si))
    kcol = ATT_WIDTH // KV_WIDTH
    smem = pl.BlockSpec(memory_space=pltpu.SMEM)
    return pl.pallas_call(
        _attn_prompt_kernel,
        grid=(bsz, nb),
        in_specs=[
            smem, smem,
            pl.BlockSpec((WINDOW, 2 * WINDOW), lambda b, n: (0, 0)),
            pl.BlockSpec((None, WINDOW, ATT_WIDTH), lambda b, n: (b, n, 0)),
            pl.BlockSpec((None, WINDOW, KV_WIDTH), lambda b, n: (b, jnp.maximum(n - 1, 0), kcol)),
            pl.BlockSpec((None, WINDOW, KV_WIDTH), lambda b, n: (b, n, kcol)),
            pl.BlockSpec((None, WINDOW, KV_WIDTH), lambda b, n: (b, jnp.maximum(n - 1, 0), kcol + 1)),
            pl.BlockSpec((None, WINDOW, KV_WIDTH), lambda b, n: (b, n, kcol + 1)),
        ],
        out_specs=pl.BlockSpec((None, WINDOW, ATT_WIDTH), lambda b, n: (b, n, 0)),
        out_shape=jax.ShapeDtypeStruct((bsz, l, ATT_WIDTH), _F32),
        scratch_shapes=[pltpu.VMEM((ATT_HEADS, WINDOW, 2 * WINDOW), _F32)],
        compiler_params=pltpu.CompilerParams(dimension_semantics=("arbitrary", "arbitrary")),
        name="attn_prompt",
    )(rel_bias.reshape(-1), sinks, bidx, att, att, att, att, att)


def _attn_sample_kernel(relb_ref, sink_ref, bidx_ref, q_ref, kvn_ref, kc_ref, vc_ref,
                        o_ref, nk_ref, nv_ref, bias_scr, sink_scr, kbuf, vbuf):
    @pl.when(pl.program_id(0) == 0)
    def _():
        bidx = bidx_ref[...]
        row_g = lax.broadcasted_iota(jnp.int32, (ATT_GROUP * DEC_SEQ, V7X_LANES), 0) // DEC_SEQ
        for kv in range(ATT_KV_HEADS):
            bias = jnp.full(bidx.shape, _NEG_INF, _F32)
            sk = jnp.zeros(row_g.shape, _F32)
            for g in range(ATT_GROUP):
                h = kv * ATT_GROUP + g
                rows = lax.broadcasted_iota(jnp.int32, bidx.shape, 0) // DEC_SEQ == g
                bias = jnp.where(rows, _bias_from_buckets(bidx, relb_ref, h), bias)
                sk = jnp.where(row_g == g, sink_ref[h], sk)
            bias_scr[kv] = bias
            sink_scr[kv] = sk
        pad = jnp.zeros((SAMPLE_KEYS_PAD - WINDOW, KV_WIDTH), _F32)
        kbuf[WINDOW:, :] = pad
        vbuf[WINDOW:, :] = pad

    for b in range(SAMPLE_GROUP):
        kvn = kvn_ref[b]
        kbuf[:WINDOW, :] = kc_ref[b]
        vbuf[:WINDOW, :] = vc_ref[b]
        kbuf[WINDOW:SAMPLE_KEYS, :] = kvn[:, :KV_WIDTH]
        vbuf[WINDOW:SAMPLE_KEYS, :] = kvn[:, KV_WIDTH:]
        nk_ref[b] = kbuf[DEC_SEQ:SAMPLE_KEYS, :]
        nv_ref[b] = vbuf[DEC_SEQ:SAMPLE_KEYS, :]
        k = kbuf[...]
        v = vbuf[...]
        for kv in range(ATT_KV_HEADS):
            ds = slice(kv * ATT_HEAD_DIM, (kv + 1) * ATT_HEAD_DIM)
            q = (q_ref[b, kv] * (ATT_HEAD_DIM ** -0.5)).astype(_BF16)
            s = _dot_nt(q, k[:, ds].astype(_BF16)) + bias_scr[kv]
            sink = sink_scr[kv][:, :1]
            m = jnp.maximum(jnp.max(s, axis=-1, keepdims=True), sink)
            p = jnp.exp(s - m)
            den = jnp.sum(p, axis=-1, keepdims=True) + jnp.exp(sink - m)
            o_ref[b, kv] = _dot(p.astype(_BF16), v[:, ds].astype(_BF16)) / den


def _attn_sample(q_r, kv_new, cache_k, cache_v, rel_bias, sinks):
    db = q_r.shape[0]
    rows = ATT_GROUP * DEC_SEQ
    t = (np.arange(rows) % DEC_SEQ)[:, None]
    c = np.arange(SAMPLE_KEYS_PAD)[None, :]
    dist = np.where(c < SAMPLE_KEYS, t + WINDOW - c, -1)
    bidx = jnp.asarray(_t5_bucket_np(dist))
    smem = pl.BlockSpec(memory_space=pltpu.SMEM)
    g = SAMPLE_GROUP
    return pl.pallas_call(
        _attn_sample_kernel,
        grid=(db // g,),
        in_specs=[
            smem, smem,
            pl.BlockSpec((rows, SAMPLE_KEYS_PAD), lambda i: (0, 0)),
            pl.BlockSpec((g, ATT_KV_HEADS, rows, ATT_HEAD_DIM), lambda i: (i, 0, 0, 0)),
            pl.BlockSpec((g, DEC_SEQ, 2 * KV_WIDTH), lambda i: (i, 0, 0)),
            pl.BlockSpec((g, WINDOW, KV_WIDTH), lambda i: (i, 0, 0)),
            pl.BlockSpec((g, WINDOW, KV_WIDTH), lambda i: (i, 0, 0)),
        ],
        out_specs=[
            pl.BlockSpec((g, ATT_KV_HEADS, rows, ATT_HEAD_DIM), lambda i: (i, 0, 0, 0)),
            pl.BlockSpec((g, WINDOW, KV_WIDTH), lambda i: (i, 0, 0)),
            pl.BlockSpec((g, WINDOW, KV_WIDTH), lambda i: (i, 0, 0)),
        ],
        out_shape=[
            jax.ShapeDtypeStruct((db, ATT_KV_HEADS, rows, ATT_HEAD_DIM), _F32),
            jax.ShapeDtypeStruct((db, WINDOW, KV_WIDTH), _F32),
            jax.ShapeDtypeStruct((db, WINDOW, KV_WIDTH), _F32),
        ],
        scratch_shapes=[
            pltpu.VMEM((ATT_KV_HEADS, rows, SAMPLE_KEYS_PAD), _F32),
            pltpu.VMEM((ATT_KV_HEADS, rows, V7X_LANES), _F32),
            pltpu.VMEM((SAMPLE_KEYS_PAD, KV_WIDTH), _F32),
            pltpu.VMEM((SAMPLE_KEYS_PAD, KV_WIDTH), _F32),
        ],
        compiler_params=pltpu.CompilerParams(dimension_semantics=("arbitrary",)),
        name="attn_sample",
    )(rel_bias.reshape(-1), sinks, bidx, q_r, kv_new, cache_k, cache_v)


def _lower_bound(lbg):
    mx = jnp.max(lbg, axis=0, keepdims=True)
    e = jnp.exp(lbg - mx)
    return e[0:1] / jnp.sum(e, axis=0, keepdims=True)


def _hgrn_gates(hg, lb):
    q = jax.nn.silu(hg[:, 0:HG_WIDTH])
    f = lb + (1.0 - lb) * jax.nn.sigmoid(hg[:, HG_WIDTH:2 * HG_WIDTH])
    v = hg[:, 2 * HG_WIDTH:3 * HG_WIDTH]
    gate = jax.nn.silu(hg[:, 3 * HG_WIDTH:4 * HG_WIDTH])
    return q, jnp.log(f), 1.0 - f, v, gate


def _hgrn_levels(c):
    return [c >> (i + 1) for i in range(int(math.log2(c)))]


def _hgrn_tables(c):
    halves = _hgrn_levels(c)
    mats = []
    lvl = np.full((c, c), -1, np.int32)
    j = np.arange(c)[None, :]
    t = np.arange(c)[:, None]
    for li, h in enumerate(halves):
        mid = (t // (2 * h)) * (2 * h) + h
        lower = t >= mid
        mats.append(np.where(lower, (j >= mid) & (j <= t), (j > t) & (j < mid)))
        same = (t // (2 * h)) == (j // (2 * h))
        lvl[same & lower & (j < mid)] = li
    lvl[np.arange(c), np.arange(c)] = len(halves)
    mats.append(j <= t)
    mats.append(j > t)
    return np.concatenate(mats, axis=0).astype(np.float32), lvl


def _exact_rows_matmul(m, x):
    h1 = x.astype(_BF16)
    r1 = x - h1.astype(_F32)
    h2 = r1.astype(_BF16)
    h3 = (r1 - h2.astype(_F32)).astype(_BF16)
    return (_dot(m, h3) + _dot(m, h2)) + _dot(m, h1)


def _hgrn_prompt_kernel(hg_ref, lbg_ref, g_ref, m_ref, lvl_ref, o_ref, s_ref, st_scr):
    i = pl.program_id(1)
    c = HG_CHUNK
    nlev = len(_hgrn_levels(c))

    @pl.when(i == 0)
    def _():
        st_scr[...] = jnp.zeros(st_scr.shape, _F32)

    lb = _lower_bound(lbg_ref[...])
    q, logf, k, v, gate = _hgrn_gates(hg_ref[...], lb)
    args = _exact_rows_matmul(m_ref[...], logf)
    lvl = lvl_ref[...]
    for h in range(HG_HEADS):
        hs = slice(h * HG_KEY, (h + 1) * HG_KEY)
        qh, kh, vh = q[:, hs], k[:, hs], v[:, hs].astype(_BF16)
        a = jnp.where(lvl == nlev, _dot_nt(qh.astype(_BF16), kh.astype(_BF16)), 0.0)
        for li in range(nlev):
            e = jnp.exp(args[li * c:(li + 1) * c, hs])
            p = _dot_nt((qh * e).astype(_BF16), (kh * e).astype(_BF16))
            a = jnp.where(lvl == li, p, a)
        cb = args[nlev * c:(nlev + 1) * c, hs]
        rem = args[(nlev + 1) * c:(nlev + 2) * c, hs]
        st = st_scr[h]
        o = _dot_nt((qh * jnp.exp(cb)).astype(_BF16), st.astype(_BF16)) + _dot(a.astype(_BF16), vh)
        kend = (kh * jnp.exp(rem)).astype(_BF16)
        st_scr[h] = st * jnp.exp(cb[c - 1:c, :]) + _dot_tn(vh, kend)
        o = o * lax.rsqrt(jnp.mean(o * o, axis=-1, keepdims=True) + RMS_EPS)
        o_ref[:, hs] = o * g_ref[:, hs] * gate[:, hs]

    @pl.when(i == pl.num_programs(1) - 1)
    def _():
        for h in range(HG_HEADS):
            s_ref[h] = st_scr[h].T


def _hgrn_prompt(hg, lb_gamma, hg_g):
    bsz, l, _ = hg.shape
    c = HG_CHUNK
    mats, lvl = _hgrn_tables(c)
    return pl.pallas_call(
        _hgrn_prompt_kernel,
        grid=(bsz, l // c),
        in_specs=[
            pl.BlockSpec((None, c, HG_SLAB), lambda b, i: (b, i, 0)),
            pl.BlockSpec(lb_gamma.shape, lambda b, i: (0, 0)),
            pl.BlockSpec((1, HG_WIDTH), lambda b, i: (0, 0)),
            pl.BlockSpec(mats.shape, lambda b, i: (0, 0)),
            pl.BlockSpec(lvl.shape, lambda b, i: (0, 0)),
        ],
        out_specs=[
            pl.BlockSpec((None, c, HG_WIDTH), lambda b, i: (b, i, 0)),
            pl.BlockSpec((None, HG_HEADS, HG_KEY, HG_VAL), lambda b, i: (b, 0, 0, 0)),
        ],
        out_shape=[
            jax.ShapeDtypeStruct((bsz, l, HG_WIDTH), _F32),
            jax.ShapeDtypeStruct((bsz, HG_HEADS, HG_KEY, HG_VAL), _F32),
        ],
        scratch_shapes=[pltpu.VMEM((HG_HEADS, HG_VAL, HG_KEY), _F32)],
        compiler_params=pltpu.CompilerParams(
            dimension_semantics=("arbitrary", "arbitrary"), vmem_limit_bytes=V7X_VMEM_LIMIT_BYTES),
        name="hgrn_prompt",
    )(hg, lb_gamma, hg_g, jnp.asarray(mats, _BF16), jnp.asarray(lvl))


def _hgrn_sample_kernel(hg_ref, lbg_ref, g_ref, mask_ref, s0_ref, o_ref, s_ref):
    nb = SAMPLE_GROUP
    lb = _lower_bound(lbg_ref[...])
    q, cb, k, v, gate = [], [], [], [], []
    for t in range(DEC_SEQ):
        qt, lf, kt, vt, gt = _hgrn_gates(hg_ref[t], lb)
        q.append(qt); k.append(kt); v.append(vt); gate.append(gt)
        cb.append(lf if t == 0 else cb[-1] + lf)
    total = cb[-1]
    mask = mask_ref[...]
    for h in range(HG_HEADS):
        hs = slice(h * HG_KEY, (h + 1) * HG_KEY)
        qdec = jnp.concatenate([q[t][:, hs] * jnp.exp(cb[t][:, hs]) for t in range(DEC_SEQ)], axis=0)
        kend = jnp.concatenate(
            [k[t][:, hs] * jnp.exp(total[:, hs] - cb[t][:, hs]) for t in range(DEC_SEQ)], axis=0)
        vall = jnp.concatenate([v[t][:, hs] for t in range(DEC_SEQ)], axis=0).astype(_BF16)
        qblk = (jnp.concatenate([qdec] * nb, axis=1) * mask).astype(_BF16)
        kblk = (jnp.concatenate([kend] * nb, axis=1) * mask).astype(_BF16)
        s0 = s0_ref[:, h]
        o_inter = _dot(qblk, s0.reshape(nb * HG_KEY, HG_VAL).astype(_BF16))
        upd = _dot_tn(kblk, vall)
        decay_t = jnp.exp(total[:, hs]).T
        for b in range(nb):
            s_ref[b, h] = s0[b] * decay_t[:, b:b + 1] + upd[b * HG_KEY:(b + 1) * HG_KEY]
        for t in range(DEC_SEQ):
            o = o_inter[t * nb:(t + 1) * nb]
            for s in range(t + 1):
                w = q[t][:, hs] * k[s][:, hs]
                if s < t:
                    w = w * jnp.exp(cb[t][:, hs] - cb[s][:, hs])
                o = o + jnp.sum(w, axis=-1, keepdims=True) * v[s][:, hs]
            o = o * lax.rsqrt(jnp.mean(o * o, axis=-1, keepdims=True) + RMS_EPS)
            o_ref[t, :, hs] = o * g_ref[:, hs] * gate[t][:, hs]


def _hgrn_sample(hg, s0, lb_gamma, hg_g):
    l, db, _ = hg.shape
    nb = SAMPLE_GROUP
    rb = np.arange(l * nb)[:, None] % nb
    cbk = np.arange(nb * HG_KEY)[None, :] // HG_KEY
    mask = jnp.asarray((rb == cbk).astype(np.float32))
    return pl.pallas_call(
        _hgrn_sample_kernel,
        grid=(db // nb,),
        in_specs=[
            pl.BlockSpec((l, nb, HG_SLAB), lambda i: (0, i, 0)),
            pl.BlockSpec(lb_gamma.shape, lambda i: (0, 0)),
            pl.BlockSpec((1, HG_WIDTH), lambda i: (0, 0)),
            pl.BlockSpec(mask.shape, lambda i: (0, 0)),
            pl.BlockSpec((nb, HG_HEADS, HG_KEY, HG_VAL), lambda i: (i, 0, 0, 0)),
        ],
        out_specs=[
            pl.BlockSpec((l, nb, HG_WIDTH), lambda i: (0, i, 0)),
            pl.BlockSpec((nb, HG_HEADS, HG_KEY, HG_VAL), lambda i: (i, 0, 0, 0)),
        ],
        out_shape=[
            jax.ShapeDtypeStruct((l, db, HG_WIDTH), _F32),
            jax.ShapeDtypeStruct((db, HG_HEADS, HG_KEY, HG_VAL), _F32),
        ],
        compiler_params=pltpu.CompilerParams(
            dimension_semantics=("arbitrary",), vmem_limit_bytes=V7X_VMEM_LIMIT_BYTES),
        name="hgrn_sample",
    )(hg, lb_gamma, hg_g, mask, s0)


def _ffn_kernel(*refs, shift, blocks_per_seq, has_prev):
    if has_prev:
        (x_ref, oa_ref, oh_ref, ag_ref, wo_ref, n2_ref, wfi_ref, cw_ref, cb_ref, wfo_ref, fg_ref,
         prev_ref, y_ref, conv_ref, buf) = refs
    else:
        (x_ref, oa_ref, oh_ref, ag_ref, wo_ref, n2_ref, wfi_ref, cw_ref, cb_ref, wfo_ref, fg_ref,
         y_ref, conv_ref, buf) = refs
    tm = x_ref.shape[0]
    pad = buf.shape[0] - tm
    i = pl.program_id(0)

    oa = _rms(oa_ref[...], ag_ref[...])
    mix = jnp.concatenate([oa.astype(_BF16), oh_ref[...].astype(_BF16)], axis=-1)
    x1 = x_ref[...] + _dot(mix, wo_ref[...])
    h2 = _rms(x1, n2_ref[...]).astype(_BF16)

    if has_prev:
        buf[:pad, :] = prev_ref[...]
    else:
        @pl.when(i % blocks_per_seq == 0)
        def _():
            buf[:pad, :] = jnp.zeros((pad, D_FF), _F32)

        @pl.when(i % blocks_per_seq != 0)
        def _():
            buf[:pad, :] = buf[tm:tm + pad, :]

    acc = x1
    for c in range(D_FF // FFN_COLS):
        cs = slice(c * FFN_COLS, (c + 1) * FFN_COLS)
        a = _dot(h2, wfi_ref[:, cs])
        gate = _dot(h2, wfi_ref[:, D_FF + c * FFN_COLS:D_FF + (c + 1) * FFN_COLS])
        buf[pad:, cs] = a
        ac = cb_ref[:, cs] + buf[pad - 2 * shift:pad - 2 * shift + tm, cs] * cw_ref[0:1, cs]
        ac = ac + buf[pad - shift:pad - shift + tm, cs] * cw_ref[1:2, cs]
        ac = ac + a * cw_ref[2:3, cs]
        hid = (jax.nn.silu(ac) * gate).astype(_BF16)
        acc = acc + _dot(hid, wfo_ref[cs, :])
    y_ref[...] = _rms(acc, fg_ref[...])
    conv_ref[...] = buf[pad + tm - 2 * shift:, :]


def _ffn(x, oa, oh, attn_g, w_o, n2, wf_in, conv_w, conv_b, wf_out, final_g, *, shift,
         rows_per_seq, prev=None):
    t = x.shape[0]
    tm = FFN_ROWS
    nblk = t // tm
    blocks_per_seq = rows_per_seq // tm
    nseq = nblk // blocks_per_seq
    pad = max(V7X_SUBLANES, 2 * shift)
    tail = 2 * shift
    const = functools.partial(pl.BlockSpec, pipeline_mode=pl.Buffered(1))
    in_specs = [
        pl.BlockSpec((tm, D_MODEL), lambda i: (i, 0)),
        pl.BlockSpec((tm, ATT_WIDTH), lambda i: (i, 0)),
        pl.BlockSpec((tm, HG_WIDTH), lambda i: (i, 0)),
        const((1, ATT_WIDTH), lambda i: (0, 0)),
        const((ATT_WIDTH + HG_WIDTH, D_MODEL), lambda i: (0, 0)),
        const((1, D_MODEL), lambda i: (0, 0)),
        const((D_MODEL, 2 * D_FF), lambda i: (0, 0)),
        const((CONV_W, D_FF), lambda i: (0, 0)),
        const((1, D_FF), lambda i: (0, 0)),
        const((D_FF, D_MODEL), lambda i: (0, 0)),
        const((1, D_MODEL), lambda i: (0, 0)),
    ]
    args = [x, oa, oh, attn_g, w_o, n2, wf_in, conv_w, conv_b, wf_out, final_g]
    if prev is not None:
        in_specs.append(const((pad, D_FF), lambda i: (0, 0)))
        args.append(prev)
    return pl.pallas_call(
        functools.partial(_ffn_kernel, shift=shift, blocks_per_seq=blocks_per_seq,
                          has_prev=prev is not None),
        grid=(nblk,),
        in_specs=in_specs,
        out_specs=[
            pl.BlockSpec((tm, D_MODEL), lambda i: (i, 0)),
            pl.BlockSpec((None, tail, D_FF), lambda i: (i // blocks_per_seq, 0, 0)),
        ],
        out_shape=[
            jax.ShapeDtypeStruct((t, D_MODEL), _F32),
            jax.ShapeDtypeStruct((nseq, tail, D_FF), _F32),
        ],
        scratch_shapes=[pltpu.VMEM((pad + tm, D_FF), _F32)],
        compiler_params=pltpu.CompilerParams(
            dimension_semantics=("arbitrary",), vmem_limit_bytes=V7X_VMEM_LIMIT_BYTES),
        name="ffn",
    )(*args)


def kernel(x_prompt, x_sample, cache_k_win, cache_v_win, state_hgrn, state_conv, norm1_g, w_in,
           attn_sinks, rel_bias, lb_gamma, attn_out_g, hg_out_g, w_out, norm2_g, w_ffn_in, conv_w,
           conv_b, w_ffn_out, final_g):
    assert DEPTH == 1
    n1 = norm1_g[0][None]
    n2 = norm2_g[0][None]
    ag = attn_out_g[0][None]
    hgg = hg_out_g[0][None]
    fg = final_g[None]
    cb = conv_b[0][None]
    cw = conv_w[0]
    sinks = attn_sinks[0]
    w_in_b = w_in[0].astype(_BF16)
    w_o_b = w_out[0].astype(_BF16)
    wf_in_b = w_ffn_in[0].astype(_BF16)
    wf_out_b = w_ffn_out[0].astype(_BF16)
    ffn_w = (ag, w_o_b, n2, wf_in_b, cw, cb, wf_out_b, fg)

    xp = x_prompt.reshape(BATCH * SEQ, D_MODEL)
    att_p, hg_p = _proj(xp, n1, w_in_b)
    att_p = att_p.reshape(BATCH, SEQ, ATT_SLAB)
    oa_p = _attn_prompt(att_p, rel_bias, sinks)
    oh_p, s_p = _hgrn_prompt(hg_p.reshape(BATCH, SEQ, HG_SLAB), lb_gamma, hgg)
    y_p, conv_p = _ffn(xp, oa_p.reshape(BATCH * SEQ, ATT_WIDTH), oh_p.reshape(BATCH * SEQ, HG_WIDTH),
                       *ffn_w, shift=1, rows_per_seq=SEQ)
    kv_tail = att_p[:, SEQ - WINDOW:, ATT_WIDTH:]
    new_k_p = kv_tail[:, :, :KV_WIDTH].reshape(1, BATCH, WINDOW, ATT_KV_HEADS, ATT_HEAD_DIM)
    new_v_p = kv_tail[:, :, KV_WIDTH:].reshape(1, BATCH, WINDOW, ATT_KV_HEADS, ATT_HEAD_DIM)

    rows_s = DEC_SEQ * DEC_BATCH
    xs = x_sample.transpose(1, 0, 2).reshape(rows_s, D_MODEL)
    att_s, hg_s = _proj(xs, n1, w_in_b)
    att_s = att_s.reshape(DEC_SEQ, DEC_BATCH, ATT_SLAB)
    q_r = att_s[:, :, :ATT_WIDTH].reshape(DEC_SEQ, DEC_BATCH, ATT_KV_HEADS, ATT_GROUP, ATT_HEAD_DIM)
    q_r = q_r.transpose(1, 2, 3, 0, 4).reshape(DEC_BATCH, ATT_KV_HEADS, ATT_GROUP * DEC_SEQ, ATT_HEAD_DIM)
    kv_new = att_s[:, :, ATT_WIDTH:].transpose(1, 0, 2)
    o16, new_k_s, new_v_s = _attn_sample(
        q_r, kv_new, cache_k_win[0].reshape(DEC_BATCH, WINDOW, KV_WIDTH),
        cache_v_win[0].reshape(DEC_BATCH, WINDOW, KV_WIDTH), rel_bias, sinks)
    oa_s = o16.reshape(DEC_BATCH, ATT_KV_HEADS, ATT_GROUP, DEC_SEQ, ATT_HEAD_DIM)
    oa_s = oa_s.transpose(3, 0, 1, 2, 4).reshape(rows_s, ATT_WIDTH)
    oh_s, s_s = _hgrn_sample(hg_s.reshape(DEC_SEQ, DEC_BATCH, HG_SLAB), state_hgrn[0], lb_gamma, hgg)
    prev = state_conv[0].transpose(1, 0, 2).reshape((CONV_W - 1) * DEC_BATCH, D_FF)
    y_s, conv_s = _ffn(xs, oa_s, oh_s.reshape(rows_s, HG_WIDTH), *ffn_w, shift=DEC_BATCH,
                       rows_per_seq=rows_s, prev=prev)
    y_s = y_s.reshape(DEC_SEQ, DEC_BATCH, D_MODEL).transpose(1, 0, 2)
    conv_s = conv_s.reshape(CONV_W - 1, DEC_BATCH, D_FF).transpose(1, 0, 2)

    cache_shape = (1, DEC_BATCH, WINDOW, ATT_KV_HEADS, ATT_HEAD_DIM)
    return (y_p.reshape(BATCH, SEQ, D_MODEL), y_s, new_k_p, new_v_p, s_p[None], conv_p[None],
            new_k_s.reshape(cache_shape), new_v_s.reshape(cache_shape), s_s[None], conv_s[None])
```

```python
import functools
import math

import jax
import jax.numpy as jnp
import numpy as np
from jax import lax
from jax.experimental import pallas as pl
from jax.experimental.pallas import tpu as pltpu

D_MODEL = 1024
BATCH = 2
SEQ = 8192
DEPTH = 1
DEC_BATCH = 128
DEC_SEQ = 4
ATT_HEAD_DIM = 64
ATT_HEADS = 8
ATT_KV_HEADS = 2
ATT_GROUP = 4
ATT_WIDTH = 512
KV_WIDTH = ATT_KV_HEADS * ATT_HEAD_DIM
WINDOW = 128
NUM_BUCKETS = 32
MAX_DISTANCE = 128
HG_KEY = 128
HG_VAL = 128
HG_HEADS = 4
HG_WIDTH = 512
D_FF = 2816
CONV_W = 3
RMS_EPS = 1e-6
ATT_SLAB = ATT_WIDTH + 2 * KV_WIDTH
HG_SLAB = 4 * HG_WIDTH
PROJ_WIDTH = ATT_SLAB + HG_SLAB

V7X_SUBLANES = 8
V7X_LANES = 128
V7X_MXU_DIM = 256
V7X_VMEM_LIMIT_BYTES = 56 * 1024 * 1024

PROJ_ROWS = 512
ATT_ROWS = 512
FFN_ROWS = 512
FFN_COLS = V7X_MXU_DIM
HG_CHUNK = 128
SAMPLE_GROUP = 8
SAMPLE_KEYS = WINDOW + DEC_SEQ
SAMPLE_KEYS_PAD = SAMPLE_KEYS + (-SAMPLE_KEYS) % V7X_SUBLANES

_F32 = jnp.float32
_BF16 = jnp.bfloat16
_NEG_INF = float("-inf")


def _dot(a, b):
    return jnp.dot(a, b, preferred_element_type=_F32)


def _dot_nt(a, b):
    return lax.dot_general(a, b, (((1,), (1,)), ((), ())), preferred_element_type=_F32)


def _dot_tn(a, b):
    return lax.dot_general(a, b, (((0,), (0,)), ((), ())), preferred_element_type=_F32)


def _rms(x, g):
    return x * lax.rsqrt(jnp.mean(x * x, axis=-1, keepdims=True) + RMS_EPS) * g


def _t5_bucket_np(dist):
    dist = np.asarray(dist)
    n = np.maximum(dist, 0)
    max_exact = NUM_BUCKETS // 2
    nf = np.maximum(n, 1).astype(np.float64)
    large = max_exact + (np.log(nf / max_exact) / math.log(MAX_DISTANCE / max_exact)
                         * (NUM_BUCKETS - max_exact)).astype(np.int32)
    large = np.minimum(large, NUM_BUCKETS - 1)
    bucket = np.where(n < max_exact, n, large)
    return np.where((dist >= 0) & (dist < WINDOW), bucket, -1).astype(np.int32)


def _bias_from_buckets(bidx, relb_ref, head):
    def body(b, acc):
        return jnp.where(bidx == b, relb_ref[b * ATT_HEADS + head], acc)
    return lax.fori_loop(0, NUM_BUCKETS, body, jnp.full(bidx.shape, _NEG_INF, _F32))


def _proj_kernel(x_ref, g_ref, w_ref, att_ref, hg_ref):
    h = _rms(x_ref[...], g_ref[...]).astype(_BF16)
    res = _dot(h, w_ref[...])
    att_ref[...] = res[:, :ATT_SLAB]
    hg_ref[...] = res[:, ATT_SLAB:]


def _proj(x, g, w):
    t = x.shape[0]
    tm = PROJ_ROWS
    return pl.pallas_call(
        _proj_kernel,
        grid=(t // tm,),
        in_specs=[
            pl.BlockSpec((tm, D_MODEL), lambda i: (i, 0)),
            pl.BlockSpec((1, D_MODEL), lambda i: (0, 0)),
            pl.BlockSpec((D_MODEL, PROJ_WIDTH), lambda i: (0, 0)),
        ],
        out_specs=[
            pl.BlockSpec((tm, ATT_SLAB), lambda i: (i, 0)),
            pl.BlockSpec((tm, HG_SLAB), lambda i: (i, 0)),
        ],
        out_shape=[
            jax.ShapeDtypeStruct((t, ATT_SLAB), _F32),
            jax.ShapeDtypeStruct((t, HG_SLAB), _F32),
        ],
        compiler_params=pltpu.CompilerParams(
            dimension_semantics=("arbitrary",), vmem_limit_bytes=V7X_VMEM_LIMIT_BYTES),
        name="proj",
    )(x, g, w)


def _attn_prompt_kernel(relb_ref, sink_ref, bidx_ref, q_ref, kvp_ref, kvc_ref, o_ref, bias_scr):
    b = pl.program_id(0)
    n = pl.program_id(1)
    w = WINDOW
    pair = 2 * ATT_HEAD_DIM
    half_g = ATT_GROUP // 2

    @pl.when((b == 0) & (n == 0))
    def _():
        bidx = bidx_ref[...]
        for kv in range(ATT_KV_HEADS):
            for par in range(2):
                for i in range(half_g):
                    h = kv * ATT_GROUP + 2 * i + par
                    bias_scr[kv, par, i * w:(i + 1) * w, :] = _bias_from_buckets(bidx, relb_ref, h)

    kvx = jnp.concatenate([kvp_ref[...], kvc_ref[...]], axis=0)
    lo = lax.broadcasted_iota(jnp.int32, (kvx.shape[0], pair), 1) < ATT_HEAD_DIM
    zero = jnp.zeros((kvx.shape[0], pair), _BF16)

    def padded(x):
        xr = pltpu.roll(x, ATT_HEAD_DIM, 1).astype(_BF16)
        x = x.astype(_BF16)
        return ((jnp.where(lo, x, zero), jnp.where(lo, xr, zero)),
                (jnp.where(lo, zero, xr), jnp.where(lo, zero, x)))

    kmat = padded(kvx[:, :pair])
    vmat = padded(kvx[:, pair:])
    col = lax.broadcasted_iota(jnp.int32, (half_g * w, 2 * w), 1)
    no_prev = (col < w) & (n == 0)
    for j in range(ATT_ROWS // w):
        qs = slice(j * w, (j + 1) * w)
        ks = slice(j * w, (j + 2) * w)
        for kv in range(ATT_KV_HEADS):
            base = kv * ATT_GROUP * ATT_HEAD_DIM
            lhs = jnp.concatenate(
                [q_ref[qs, base + i * pair:base + (i + 1) * pair] for i in range(half_g)], axis=0)
            lhs = (lhs * (ATT_HEAD_DIM ** -0.5)).astype(_BF16)
            o_pair = None
            for par in range(2):
                s = _dot_nt(lhs, kmat[par][kv][ks]) + bias_scr[kv, par]
                if j == 0:
                    s = jnp.where(no_prev, _NEG_INF, s)
                smax = jnp.max(s, axis=-1, keepdims=True)
                sink = jnp.concatenate(
                    [jnp.full((w, 1), sink_ref[kv * ATT_GROUP + 2 * i + par], _F32)
                     for i in range(half_g)], axis=0)
                m = jnp.maximum(smax, sink)
                p = jnp.exp(s - m)
                den = jnp.sum(p, axis=-1, keepdims=True) + jnp.exp(sink - m)
                o = _dot(p.astype(_BF16), vmat[par][kv][ks]) / den
                o_pair = o if o_pair is None else o_pair + o
            for i in range(half_g):
                o_ref[qs, base + i * pair:base + (i + 1) * pair] = o_pair[i * w:(i + 1) * w]


def _attn_prompt(att, rel_bias, sinks):
    bsz, l, _ = att.shape
    rows = ATT_ROWS
    qi = np.arange(WINDOW)[:, None]
    si = np.arange(2 * WINDOW)[None, :]
    bidx = jnp.asarray(_t5_bucket_np(WINDOW + qi - si))
    kvcol = ATT_WIDTH // (2 * KV_WIDTH)
    blocks = rows // WINDOW
    smem = pl.BlockSpec(memory_space=pltpu.SMEM)
    return pl.pallas_call(
        _attn_prompt_kernel,
        grid=(bsz, l // rows),
        in_specs=[
            smem, smem,
            pl.BlockSpec((WINDOW, 2 * WINDOW), lambda b, n: (0, 0)),
            pl.BlockSpec((None, rows, ATT_WIDTH), lambda b, n: (b, n, 0)),
            pl.BlockSpec((None, WINDOW, 2 * KV_WIDTH),
                         lambda b, n: (b, jnp.maximum(n * blocks - 1, 0), kvcol)),
            pl.BlockSpec((None, rows, 2 * KV_WIDTH), lambda b, n: (b, n, kvcol)),
        ],
        out_specs=pl.BlockSpec((None, rows, ATT_WIDTH), lambda b, n: (b, n, 0)),
        out_shape=jax.ShapeDtypeStruct((bsz, l, ATT_WIDTH), _F32),
        scratch_shapes=[pltpu.VMEM((ATT_KV_HEADS, 2, ATT_GROUP // 2 * WINDOW, 2 * WINDOW), _F32)],
        compiler_params=pltpu.CompilerParams(dimension_semantics=("arbitrary", "arbitrary")),
        name="attn_prompt",
    )(rel_bias.reshape(-1), sinks, bidx, att, att, att)


def _attn_sample_kernel(relb_ref, sink_ref, bidx_ref, q_ref, kvn_ref, kc_ref, vc_ref,
                        o_ref, nk_ref, nv_ref, bias_scr, sink_scr):
    @pl.when(pl.program_id(0) == 0)
    def _():
        bidx = bidx_ref[...]
        row_g = lax.broadcasted_iota(jnp.int32, (ATT_GROUP * DEC_SEQ, V7X_LANES), 0) // DEC_SEQ
        for kv in range(ATT_KV_HEADS):
            bias = jnp.full(bidx.shape, _NEG_INF, _F32)
            sk = jnp.zeros(row_g.shape, _F32)
            for g in range(ATT_GROUP):
                h = kv * ATT_GROUP + g
                rows = lax.broadcasted_iota(jnp.int32, bidx.shape, 0) // DEC_SEQ == g
                bias = jnp.where(rows, _bias_from_buckets(bidx, relb_ref, h), bias)
                sk = jnp.where(row_g == g, sink_ref[h], sk)
            bias_scr[kv] = bias
            sink_scr[kv] = sk

    keep = WINDOW - DEC_SEQ
    units = [(b, kv) for b in range(SAMPLE_GROUP) for kv in range(ATT_KV_HEADS)]
    scores = {}
    for b in range(SAMPLE_GROUP):
        kvn = kvn_ref[b]
        nk_ref[b, :keep, :] = kc_ref[b, DEC_SEQ:, :]
        nv_ref[b, :keep, :] = vc_ref[b, DEC_SEQ:, :]
        nk_ref[b, keep:, :] = kvn[:DEC_SEQ, :KV_WIDTH]
        nv_ref[b, keep:, :] = kvn[:DEC_SEQ, KV_WIDTH:]
        k = jnp.concatenate([kc_ref[b], kvn[:, :KV_WIDTH]], axis=0).astype(_BF16)
        for kv in range(ATT_KV_HEADS):
            q = (q_ref[b, kv] * (ATT_HEAD_DIM ** -0.5)).astype(_BF16)
            scores[b, kv] = _dot_nt(q, k)
    probs = {}
    for b, kv in units:
        s = scores[b, kv] + bias_scr[kv]
        sink = sink_scr[kv][:, :1]
        m = jnp.maximum(jnp.max(s, axis=-1, keepdims=True), sink)
        p = jnp.exp(s - m)
        den = jnp.sum(p, axis=-1, keepdims=True) + jnp.exp(sink - m)
        probs[b, kv] = (p.astype(_BF16), den)
    for b in range(SAMPLE_GROUP):
        v = jnp.concatenate([vc_ref[b], kvn_ref[b][:, KV_WIDTH:]], axis=0).astype(_BF16)
        for kv in range(ATT_KV_HEADS):
            p, den = probs[b, kv]
            o = _dot(p, v) / den
            o_ref[b, kv] = o[:, kv * ATT_HEAD_DIM:(kv + 1) * ATT_HEAD_DIM]


def _attn_sample(q_r, kv_new, cache_k, cache_v, rel_bias, sinks):
    db = q_r.shape[0]
    rows = ATT_GROUP * DEC_SEQ
    new_rows = SAMPLE_KEYS_PAD - WINDOW
    t = (np.arange(rows) % DEC_SEQ)[:, None]
    c = np.arange(SAMPLE_KEYS_PAD)[None, :]
    dist = np.where(c < SAMPLE_KEYS, t + WINDOW - c, -1)
    bidx = jnp.asarray(_t5_bucket_np(dist))
    smem = pl.BlockSpec(memory_space=pltpu.SMEM)
    g = SAMPLE_GROUP
    return pl.pallas_call(
        _attn_sample_kernel,
        grid=(db // g,),
        in_specs=[
            smem, smem,
            pl.BlockSpec((rows, SAMPLE_KEYS_PAD), lambda i: (0, 0)),
            pl.BlockSpec((g, ATT_KV_HEADS, rows, KV_WIDTH), lambda i: (i, 0, 0, 0)),
            pl.BlockSpec((g, new_rows, 2 * KV_WIDTH), lambda i: (i, 0, 0)),
            pl.BlockSpec((g, WINDOW, KV_WIDTH), lambda i: (i, 0, 0)),
            pl.BlockSpec((g, WINDOW, KV_WIDTH), lambda i: (i, 0, 0)),
        ],
        out_specs=[
            pl.BlockSpec((g, ATT_KV_HEADS, rows, ATT_HEAD_DIM), lambda i: (i, 0, 0, 0)),
            pl.BlockSpec((g, WINDOW, KV_WIDTH), lambda i: (i, 0, 0)),
            pl.BlockSpec((g, WINDOW, KV_WIDTH), lambda i: (i, 0, 0)),
        ],
        out_shape=[
            jax.ShapeDtypeStruct((db, ATT_KV_HEADS, rows, ATT_HEAD_DIM), _F32),
            jax.ShapeDtypeStruct((db, WINDOW, KV_WIDTH), _F32),
            jax.ShapeDtypeStruct((db, WINDOW, KV_WIDTH), _F32),
        ],
        scratch_shapes=[
            pltpu.VMEM((ATT_KV_HEADS, rows, SAMPLE_KEYS_PAD), _F32),
            pltpu.VMEM((ATT_KV_HEADS, rows, V7X_LANES), _F32),
        ],
        compiler_params=pltpu.CompilerParams(dimension_semantics=("arbitrary",)),
        name="attn_sample",
    )(rel_bias.reshape(-1), sinks, bidx, q_r, kv_new, cache_k, cache_v)


def _lower_bound(lbg):
    mx = jnp.max(lbg, axis=0, keepdims=True)
    e = jnp.exp(lbg - mx)
    return e[0:1] / jnp.sum(e, axis=0, keepdims=True)


def _hgrn_gates(hg, lb):
    q = jax.nn.silu(hg[:, 0:HG_WIDTH])
    f = lb + (1.0 - lb) * jax.nn.sigmoid(hg[:, HG_WIDTH:2 * HG_WIDTH])
    v = hg[:, 2 * HG_WIDTH:3 * HG_WIDTH]
    gate = jax.nn.silu(hg[:, 3 * HG_WIDTH:4 * HG_WIDTH])
    return q, jnp.log(f), 1.0 - f, v, gate


def _hgrn_levels(c):
    return [c >> (i + 1) for i in range(int(math.log2(c)))]


def _hgrn_tables(c):
    lvl = np.full((c, c), -1, np.int32)
    j = np.arange(c)[None, :]
    t = np.arange(c)[:, None]
    for li, h in enumerate(_hgrn_levels(c)):
        mid = (t // (2 * h)) * (2 * h) + h
        same = (t // (2 * h)) == (j // (2 * h))
        lvl[same & (t >= mid) & (j < mid)] = li
    return (j <= t).astype(np.float32), np.concatenate([lvl, lvl], axis=1)


def _exact_rows_matmul(m, x):
    h1 = x.astype(_BF16)
    r1 = x - h1.astype(_F32)
    h2 = r1.astype(_BF16)
    h3 = (r1 - h2.astype(_F32)).astype(_BF16)
    return (_dot(m, h3) + _dot(m, h2)) + _dot(m, h1)


def _hgrn_level_exponents(logf, cb, cb_ref, c):
    row = lax.broadcasted_iota(jnp.int32, logf.shape, 0)
    args = []
    for h in _hgrn_levels(c):
        if (2 * h) % V7X_SUBLANES == 0:
            mid_cb = jnp.concatenate(
                [jnp.broadcast_to(cb_ref[pl.ds(j * 2 * h + h - 1, 1), :], (2 * h, logf.shape[1]))
                 for j in range(c // (2 * h))], axis=0)
            args.append(jnp.where((row & h) != 0, cb - mid_cb, mid_cb - cb))
        elif h == 2:
            nxt = pltpu.roll(logf, c - 1, 0)
            prv = pltpu.roll(logf, 1, 0)
            r = row & 3
            args.append(jnp.where(r == 2, logf, jnp.where(r == 3, logf + prv,
                                                          jnp.where(r == 0, nxt, 0.0))))
        else:
            assert h == 1
            args.append(jnp.where((row & 1) == 1, logf, 0.0))
    return args


def _pair_blockdiag(x):
    left = lax.broadcasted_iota(jnp.int32, x.shape, 1) < HG_KEY
    zero = jnp.zeros_like(x)
    return jnp.concatenate([jnp.where(left, x, zero), jnp.where(left, zero, x)], axis=0)


def _hgrn_prompt_kernel(hg_ref, lbg_ref, g_ref, tri_ref, lvl_ref, o_ref, s_ref, st_scr, cb_scr):
    i = pl.program_id(1)
    c = HG_CHUNK
    pw = 2 * HG_KEY

    @pl.when(i == 0)
    def _():
        st_scr[...] = jnp.zeros(st_scr.shape, _F32)

    lb = _lower_bound(lbg_ref[...])
    q, logf, k, v, gate = _hgrn_gates(hg_ref[...], lb)
    cb = _exact_rows_matmul(tri_ref[...], logf)
    cb_scr[...] = cb
    args = _hgrn_level_exponents(logf, cb, cb_scr, c)
    total = cb[c - 1:c, :]
    qdec = q * jnp.exp(cb)
    kend = k * jnp.exp(total - cb)
    qk = q * k
    lvl = lvl_ref[...]
    rr = lax.broadcasted_iota(jnp.int32, (pw, pw), 0) < HG_VAL
    cc = lax.broadcasted_iota(jnp.int32, (pw, pw), 1) < HG_KEY
    same_head = rr == cc
    for p in range(HG_HEADS // 2):
        ps = slice(p * pw, (p + 1) * pw)
        qp, kp = q[:, ps], k[:, ps]
        vp = v[:, ps].astype(_BF16)
        a = jnp.zeros((c, pw), _F32)
        for li, arg in enumerate(args):
            e = jnp.exp(arg[:, ps])
            pr = _dot_nt((qp * e).astype(_BF16), _pair_blockdiag((kp * e).astype(_BF16)))
            a = jnp.where(lvl == li, pr, a)
        st = st_scr[p]
        o = _dot_nt(qdec[:, ps].astype(_BF16), st.astype(_BF16))
        o = o + _dot(a.astype(_BF16), _pair_blockdiag(vp))
        upd = _dot_tn(vp, kend[:, ps].astype(_BF16))
        st_scr[p] = st * jnp.exp(total[:, ps]) + jnp.where(same_head, upd, 0.0)
        for hh in range(2):
            hs = slice(hh * HG_KEY, (hh + 1) * HG_KEY)
            gs = slice(p * pw + hh * HG_KEY, p * pw + (hh + 1) * HG_KEY)
            oh = o[:, hs] + jnp.sum(qk[:, gs], axis=-1, keepdims=True) * v[:, gs]
            oh = oh * lax.rsqrt(jnp.mean(oh * oh, axis=-1, keepdims=True) + RMS_EPS)
            o_ref[:, gs] = oh * g_ref[:, gs] * gate[:, gs]

    @pl.when(i == pl.num_programs(1) - 1)
    def _():
        for p in range(HG_HEADS // 2):
            st = st_scr[p]
            s_ref[2 * p] = st[:HG_VAL, :HG_KEY].T
            s_ref[2 * p + 1] = st[HG_VAL:, HG_KEY:].T


def _hgrn_prompt(hg, lb_gamma, hg_g):
    bsz, l, _ = hg.shape
    c = HG_CHUNK
    tri, lvl = _hgrn_tables(c)
    return pl.pallas_call(
        _hgrn_prompt_kernel,
        grid=(bsz, l // c),
        in_specs=[
            pl.BlockSpec((None, c, HG_SLAB), lambda b, i: (b, i, 0)),
            pl.BlockSpec(lb_gamma.shape, lambda b, i: (0, 0)),
            pl.BlockSpec((1, HG_WIDTH), lambda b, i: (0, 0)),
            pl.BlockSpec(tri.shape, lambda b, i: (0, 0)),
            pl.BlockSpec(lvl.shape, lambda b, i: (0, 0)),
        ],
        out_specs=[
            pl.BlockSpec((None, c, HG_WIDTH), lambda b, i: (b, i, 0)),
            pl.BlockSpec((None, HG_HEADS, HG_KEY, HG_VAL), lambda b, i: (b, 0, 0, 0)),
        ],
        out_shape=[
            jax.ShapeDtypeStruct((bsz, l, HG_WIDTH), _F32),
            jax.ShapeDtypeStruct((bsz, HG_HEADS, HG_KEY, HG_VAL), _F32),
        ],
        scratch_shapes=[
            pltpu.VMEM((HG_HEADS // 2, 2 * HG_VAL, 2 * HG_KEY), _F32),
            pltpu.VMEM((c, HG_WIDTH), _F32),
        ],
        compiler_params=pltpu.CompilerParams(
            dimension_semantics=("arbitrary", "arbitrary"), vmem_limit_bytes=V7X_VMEM_LIMIT_BYTES),
        name="hgrn_prompt",
    )(hg, lb_gamma, hg_g, jnp.asarray(tri, _BF16), jnp.asarray(lvl))


def _hgrn_sample_kernel(hg_ref, lbg_ref, g_ref, mask_ref, s0_ref, o_ref, s_ref):
    nb = SAMPLE_GROUP
    lb = _lower_bound(lbg_ref[...])
    q, cb, k, v, gate = [], [], [], [], []
    for t in range(DEC_SEQ):
        qt, lf, kt, vt, gt = _hgrn_gates(hg_ref[t], lb)
        q.append(qt); k.append(kt); v.append(vt); gate.append(gt)
        cb.append(lf if t == 0 else cb[-1] + lf)
    total = cb[-1]
    mask = mask_ref[...]
    for h in range(HG_HEADS):
        hs = slice(h * HG_KEY, (h + 1) * HG_KEY)
        qdec = jnp.concatenate([q[t][:, hs] * jnp.exp(cb[t][:, hs]) for t in range(DEC_SEQ)], axis=0)
        kend = jnp.concatenate(
            [k[t][:, hs] * jnp.exp(total[:, hs] - cb[t][:, hs]) for t in range(DEC_SEQ)], axis=0)
        vall = jnp.concatenate([v[t][:, hs] for t in range(DEC_SEQ)], axis=0).astype(_BF16)
        qblk = (jnp.concatenate([qdec] * nb, axis=1) * mask).astype(_BF16)
        kblk = (jnp.concatenate([kend] * nb, axis=1) * mask).astype(_BF16)
        s0 = s0_ref[:, h]
        o_inter = _dot(qblk, s0.reshape(nb * HG_KEY, HG_VAL).astype(_BF16))
        upd = _dot_tn(kblk, vall)
        decay_t = jnp.exp(total[:, hs]).T
        for b in range(nb):
            s_ref[b, h] = s0[b] * decay_t[:, b:b + 1] + upd[b * HG_KEY:(b + 1) * HG_KEY]
        for t in range(DEC_SEQ):
            o = o_inter[t * nb:(t + 1) * nb]
            for s in range(t + 1):
                w = q[t][:, hs] * k[s][:, hs]
                if s < t:
                    w = w * jnp.exp(cb[t][:, hs] - cb[s][:, hs])
                o = o + jnp.sum(w, axis=-1, keepdims=True) * v[s][:, hs]
            o = o * lax.rsqrt(jnp.mean(o * o, axis=-1, keepdims=True) + RMS_EPS)
            o_ref[t, :, hs] = o * g_ref[:, hs] * gate[t][:, hs]


def _hgrn_sample(hg, s0, lb_gamma, hg_g):
    l, db, _ = hg.shape
    nb = SAMPLE_GROUP
    rb = np.arange(l * nb)[:, None] % nb
    cbk = np.arange(nb * HG_KEY)[None, :] // HG_KEY
    mask = jnp.asarray((rb == cbk).astype(np.float32))
    return pl.pallas_call(
        _hgrn_sample_kernel,
        grid=(db // nb,),
        in_specs=[
            pl.BlockSpec((l, nb, HG_SLAB), lambda i: (0, i, 0)),
            pl.BlockSpec(lb_gamma.shape, lambda i: (0, 0)),
            pl.BlockSpec((1, HG_WIDTH), lambda i: (0, 0)),
            pl.BlockSpec(mask.shape, lambda i: (0, 0)),
            pl.BlockSpec((nb, HG_HEADS, HG_KEY, HG_VAL), lambda i: (i, 0, 0, 0)),
        ],
        out_specs=[
            pl.BlockSpec((l, nb, HG_WIDTH), lambda i: (0, i, 0)),
            pl.BlockSpec((nb, HG_HEADS, HG_KEY, HG_VAL), lambda i: (i, 0, 0, 0)),
        ],
        out_shape=[
            jax.ShapeDtypeStruct((l, db, HG_WIDTH), _F32),
            jax.ShapeDtypeStruct((db, HG_HEADS, HG_KEY, HG_VAL), _F32),
        ],
        compiler_params=pltpu.CompilerParams(
            dimension_semantics=("arbitrary",), vmem_limit_bytes=V7X_VMEM_LIMIT_BYTES),
        name="hgrn_sample",
    )(hg, lb_gamma, hg_g, mask, s0)


def _ffn_kernel(*refs, shift, blocks_per_seq, has_prev):
    if has_prev:
        (x_ref, oa_ref, oh_ref, ag_ref, wo_ref, n2_ref, wfi_ref, cw_ref, cb_ref, wfo_ref, fg_ref,
         prev_ref, y_ref, conv_ref, buf) = refs
    else:
        (x_ref, oa_ref, oh_ref, ag_ref, wo_ref, n2_ref, wfi_ref, cw_ref, cb_ref, wfo_ref, fg_ref,
         y_ref, conv_ref, buf) = refs
    tm = x_ref.shape[0]
    pad = buf.shape[0] - tm
    i = pl.program_id(0)

    oa = _rms(oa_ref[...], ag_ref[...])
    mix = jnp.concatenate([oa.astype(_BF16), oh_ref[...].astype(_BF16)], axis=-1)
    x1 = x_ref[...] + _dot(mix, wo_ref[...])
    h2 = _rms(x1, n2_ref[...]).astype(_BF16)

    if has_prev:
        buf[:pad, :] = prev_ref[...]
    else:
        @pl.when(i % blocks_per_seq == 0)
        def _():
            buf[:pad, :] = jnp.zeros((pad, D_FF), _F32)

        @pl.when(i % blocks_per_seq != 0)
        def _():
            buf[:pad, :] = buf[tm:tm + pad, :]

    acc = x1
    for c in range(D_FF // FFN_COLS):
        cs = slice(c * FFN_COLS, (c + 1) * FFN_COLS)
        a = _dot(h2, wfi_ref[:, cs])
        gate = _dot(h2, wfi_ref[:, D_FF + c * FFN_COLS:D_FF + (c + 1) * FFN_COLS])
        buf[pad:, cs] = a
        ac = cb_ref[:, cs] + buf[pad - 2 * shift:pad - 2 * shift + tm, cs] * cw_ref[0:1, cs]
        ac = ac + buf[pad - shift:pad - shift + tm, cs] * cw_ref[1:2, cs]
        ac = ac + a * cw_ref[2:3, cs]
        hid = (jax.nn.silu(ac) * gate).astype(_BF16)
        acc = acc + _dot(hid, wfo_ref[cs, :])
    y_ref[...] = _rms(acc, fg_ref[...])
    conv_ref[...] = buf[pad + tm - 2 * shift:, :]


def _ffn(x, oa, oh, attn_g, w_o, n2, wf_in, conv_w, conv_b, wf_out, final_g, *, shift,
         rows_per_seq, prev=None):
    t = x.shape[0]
    tm = FFN_ROWS
    nblk = t // tm
    blocks_per_seq = rows_per_seq // tm
    nseq = nblk // blocks_per_seq
    pad = max(V7X_SUBLANES, 2 * shift)
    tail = 2 * shift
    const = functools.partial(pl.BlockSpec, pipeline_mode=pl.Buffered(1))
    in_specs = [
        pl.BlockSpec((tm, D_MODEL), lambda i: (i, 0)),
        pl.BlockSpec((tm, ATT_WIDTH), lambda i: (i, 0)),
        pl.BlockSpec((tm, HG_WIDTH), lambda i: (i, 0)),
        const((1, ATT_WIDTH), lambda i: (0, 0)),
        const((ATT_WIDTH + HG_WIDTH, D_MODEL), lambda i: (0, 0)),
        const((1, D_MODEL), lambda i: (0, 0)),
        const((D_MODEL, 2 * D_FF), lambda i: (0, 0)),
        const((CONV_W, D_FF), lambda i: (0, 0)),
        const((1, D_FF), lambda i: (0, 0)),
        const((D_FF, D_MODEL), lambda i: (0, 0)),
        const((1, D_MODEL), lambda i: (0, 0)),
    ]
    args = [x, oa, oh, attn_g, w_o, n2, wf_in, conv_w, conv_b, wf_out, final_g]
    if prev is not None:
        in_specs.append(const((pad, D_FF), lambda i: (0, 0)))
        args.append(prev)
    return pl.pallas_call(
        functools.partial(_ffn_kernel, shift=shift, blocks_per_seq=blocks_per_seq,
                          has_prev=prev is not None),
        grid=(nblk,),
        in_specs=in_specs,
        out_specs=[
            pl.BlockSpec((tm, D_MODEL), lambda i: (i, 0)),
            pl.BlockSpec((None, tail, D_FF), lambda i: (i // blocks_per_seq, 0, 0)),
        ],
        out_shape=[
            jax.ShapeDtypeStruct((t, D_MODEL), _F32),
            jax.ShapeDtypeStruct((nseq, tail, D_FF), _F32),
        ],
        scratch_shapes=[pltpu.VMEM((pad + tm, D_FF), _F32)],
        compiler_params=pltpu.CompilerParams(
            dimension_semantics=("arbitrary",), vmem_limit_bytes=V7X_VMEM_LIMIT_BYTES),
        name="ffn",
    )(*args)


def kernel(x_prompt, x_sample, cache_k_win, cache_v_win, state_hgrn, state_conv, norm1_g, w_in,
           attn_sinks, rel_bias, lb_gamma, attn_out_g, hg_out_g, w_out, norm2_g, w_ffn_in, conv_w,
           conv_b, w_ffn_out, final_g):
    assert DEPTH == 1
    n1 = norm1_g[0][None]
    n2 = norm2_g[0][None]
    ag = attn_out_g[0][None]
    hgg = hg_out_g[0][None]
    fg = final_g[None]
    cb = conv_b[0][None]
    cw = conv_w[0]
    sinks = attn_sinks[0]
    w_in_b = w_in[0].astype(_BF16)
    w_o_b = w_out[0].astype(_BF16)
    wf_in_b = w_ffn_in[0].astype(_BF16)
    wf_out_b = w_ffn_out[0].astype(_BF16)
    ffn_w = (ag, w_o_b, n2, wf_in_b, cw, cb, wf_out_b, fg)

    xp = x_prompt.reshape(BATCH * SEQ, D_MODEL)
    att_p, hg_p = _proj(xp, n1, w_in_b)
    att_p = att_p.reshape(BATCH, SEQ, ATT_SLAB)
    oa_p = _attn_prompt(att_p, rel_bias, sinks)
    oh_p, s_p = _hgrn_prompt(hg_p.reshape(BATCH, SEQ, HG_SLAB), lb_gamma, hgg)
    y_p, conv_p = _ffn(xp, oa_p.reshape(BATCH * SEQ, ATT_WIDTH), oh_p.reshape(BATCH * SEQ, HG_WIDTH),
                       *ffn_w, shift=1, rows_per_seq=SEQ)
    kv_tail = att_p[:, SEQ - WINDOW:, ATT_WIDTH:]
    new_k_p = kv_tail[:, :, :KV_WIDTH].reshape(1, BATCH, WINDOW, ATT_KV_HEADS, ATT_HEAD_DIM)
    new_v_p = kv_tail[:, :, KV_WIDTH:].reshape(1, BATCH, WINDOW, ATT_KV_HEADS, ATT_HEAD_DIM)

    rows_s = DEC_SEQ * DEC_BATCH
    xs = x_sample.transpose(1, 0, 2).reshape(rows_s, D_MODEL)
    att_s, hg_s = _proj(xs, n1, w_in_b)
    att_s = att_s.reshape(DEC_SEQ, DEC_BATCH, ATT_SLAB)
    q_r = att_s[:, :, :ATT_WIDTH].reshape(DEC_SEQ, DEC_BATCH, ATT_KV_HEADS, ATT_GROUP, ATT_HEAD_DIM)
    q_r = q_r.transpose(1, 2, 3, 0, 4).reshape(DEC_BATCH, ATT_KV_HEADS, ATT_GROUP * DEC_SEQ, ATT_HEAD_DIM)
    q_zero = jnp.zeros_like(q_r[:, 0])
    q_r = jnp.stack([jnp.concatenate([q_r[:, 0], q_zero], axis=-1),
                     jnp.concatenate([q_zero, q_r[:, 1]], axis=-1)], axis=1)
    kv_new = att_s[:, :, ATT_WIDTH:].transpose(1, 0, 2)
    kv_new = jnp.pad(kv_new, ((0, 0), (0, SAMPLE_KEYS_PAD - SAMPLE_KEYS), (0, 0)))
    o16, new_k_s, new_v_s = _attn_sample(
        q_r, kv_new, cache_k_win[0].reshape(DEC_BATCH, WINDOW, KV_WIDTH),
        cache_v_win[0].reshape(DEC_BATCH, WINDOW, KV_WIDTH), rel_bias, sinks)
    oa_s = o16.reshape(DEC_BATCH, ATT_KV_HEADS, ATT_GROUP, DEC_SEQ, ATT_HEAD_DIM)
    oa_s = oa_s.transpose(3, 0, 1, 2, 4).reshape(rows_s, ATT_WIDTH)
    oh_s, s_s = _hgrn_sample(hg_s.reshape(DEC_SEQ, DEC_BATCH, HG_SLAB), state_hgrn[0], lb_gamma, hgg)
    prev = state_conv[0].transpose(1, 0, 2).reshape((CONV_W - 1) * DEC_BATCH, D_FF)
    y_s, conv_s = _ffn(xs, oa_s, oh_s.reshape(rows_s, HG_WIDTH), *ffn_w, shift=DEC_BATCH,
                       rows_per_seq=rows_s, prev=prev)
    y_s = y_s.reshape(DEC_SEQ, DEC_BATCH, D_MODEL).transpose(1, 0, 2)
    conv_s = conv_s.reshape(CONV_W - 1, DEC_BATCH, D_FF).transpose(1, 0, 2)

    cache_shape = (1, DEC_BATCH, WINDOW, ATT_KV_HEADS, ATT_HEAD_DIM)
    return (y_p.reshape(BATCH, SEQ, D_MODEL), y_s, new_k_p, new_v_p, s_p[None], conv_p[None],
            new_k_s.reshape(cache_shape), new_v_s.reshape(cache_shape), s_s[None], conv_s[None])
```

```python
import functools
import math

import jax
import jax.numpy as jnp
import numpy as np
from jax import lax
from jax.experimental import pallas as pl
from jax.experimental.pallas import tpu as pltpu

D_MODEL = 1024
BATCH = 2
SEQ = 8192
DEPTH = 1
DEC_BATCH = 128
DEC_SEQ = 4
ATT_HEAD_DIM = 64
ATT_HEADS = 8
ATT_KV_HEADS = 2
ATT_GROUP = 4
ATT_WIDTH = 512
KV_WIDTH = ATT_KV_HEADS * ATT_HEAD_DIM
WINDOW = 128
NUM_BUCKETS = 32
MAX_DISTANCE = 128
HG_KEY = 128
HG_VAL = 128
HG_HEADS = 4
HG_WIDTH = 512
D_FF = 2816
CONV_W = 3
RMS_EPS = 1e-6
ATT_SLAB = ATT_WIDTH + 2 * KV_WIDTH
HG_SLAB = 4 * HG_WIDTH
PROJ_WIDTH = ATT_SLAB + HG_SLAB

V7X_SUBLANES = 8
V7X_LANES = 128
V7X_MXU_DIM = 256
V7X_VMEM_LIMIT_BYTES = 56 * 1024 * 1024

PROJ_ROWS = 512
ATT_ROWS = 512
FFN_ROWS = 512
FFN_COLS = V7X_MXU_DIM
HG_CHUNK = 128
SAMPLE_GROUP = 8
SAMPLE_KEYS = WINDOW + DEC_SEQ
SAMPLE_KEYS_PAD = SAMPLE_KEYS + (-SAMPLE_KEYS) % V7X_SUBLANES

_F32 = jnp.float32
_BF16 = jnp.bfloat16
_NEG_INF = float("-inf")
_LOG2_E = 1.0 / math.log(2.0)


def _dot(a, b):
    return jnp.dot(a, b, preferred_element_type=_F32)


def _dot_nt(a, b):
    return lax.dot_general(a, b, (((1,), (1,)), ((), ())), preferred_element_type=_F32)


def _dot_tn(a, b):
    return lax.dot_general(a, b, (((0,), (0,)), ((), ())), preferred_element_type=_F32)


def _rms(x, g):
    return x * lax.rsqrt(jnp.mean(x * x, axis=-1, keepdims=True) + RMS_EPS) * g


def _t5_bucket_np(dist):
    dist = np.asarray(dist)
    n = np.maximum(dist, 0)
    max_exact = NUM_BUCKETS // 2
    nf = np.maximum(n, 1).astype(np.float64)
    large = max_exact + (np.log(nf / max_exact) / math.log(MAX_DISTANCE / max_exact)
                         * (NUM_BUCKETS - max_exact)).astype(np.int32)
    large = np.minimum(large, NUM_BUCKETS - 1)
    bucket = np.where(n < max_exact, n, large)
    return np.where((dist >= 0) & (dist < WINDOW), bucket, -1).astype(np.int32)


def _bias_from_buckets(bidx, relb_ref, head):
    def body(b, acc):
        return jnp.where(bidx == b, relb_ref[b * ATT_HEADS + head], acc)
    return lax.fori_loop(0, NUM_BUCKETS, body, jnp.full(bidx.shape, _NEG_INF, _F32))


def _proj_kernel(x_ref, g_ref, w_ref, att_ref, hg_ref):
    h = _rms(x_ref[...], g_ref[...]).astype(_BF16)
    res = _dot(h, w_ref[...])
    att_ref[...] = res[:, :ATT_SLAB]
    hg_ref[...] = res[:, ATT_SLAB:]


def _proj(x, g, w):
    t = x.shape[0]
    tm = PROJ_ROWS
    return pl.pallas_call(
        _proj_kernel,
        grid=(t // tm,),
        in_specs=[
            pl.BlockSpec((tm, D_MODEL), lambda i: (i, 0)),
            pl.BlockSpec((1, D_MODEL), lambda i: (0, 0)),
            pl.BlockSpec((D_MODEL, PROJ_WIDTH), lambda i: (0, 0)),
        ],
        out_specs=[
            pl.BlockSpec((tm, ATT_SLAB), lambda i: (i, 0)),
            pl.BlockSpec((tm, HG_SLAB), lambda i: (i, 0)),
        ],
        out_shape=[
            jax.ShapeDtypeStruct((t, ATT_SLAB), _F32),
            jax.ShapeDtypeStruct((t, HG_SLAB), _F32),
        ],
        compiler_params=pltpu.CompilerParams(
            dimension_semantics=("arbitrary",), vmem_limit_bytes=V7X_VMEM_LIMIT_BYTES),
        name="proj",
    )(x, g, w)


def _attn_prompt_kernel(relb_ref, sink_ref, bidx_ref, q_ref, kvp_ref, kvc_ref, o_ref, bias_scr):
    b = pl.program_id(0)
    n = pl.program_id(1)
    w = WINDOW
    pair = 2 * ATT_HEAD_DIM
    half_g = ATT_GROUP // 2

    @pl.when((b == 0) & (n == 0))
    def _():
        bidx = bidx_ref[...]
        key0 = lax.broadcasted_iota(jnp.int32, bidx.shape, 0) == 0
        for kv in range(ATT_KV_HEADS):
            for par in range(2):
                for i in range(half_g):
                    h = kv * ATT_GROUP + 2 * i + par
                    tab = _bias_from_buckets(bidx, relb_ref, h)
                    bias_scr[kv, par, :, i * w:(i + 1) * w] = jnp.where(key0, sink_ref[h], tab)

    kvx = jnp.concatenate([kvp_ref[...], kvc_ref[...]], axis=0)
    kx = kvx[:, :pair]
    kr = pltpu.roll(kx, ATT_HEAD_DIM, 1).astype(_BF16)
    kx = kx.astype(_BF16)
    lo = lax.broadcasted_iota(jnp.int32, kx.shape, 1) < ATT_HEAD_DIM
    zero = jnp.zeros_like(kx)
    kmat = ((jnp.where(lo, kx, zero), jnp.where(lo, kr, zero)),
            (jnp.where(lo, zero, kr), jnp.where(lo, zero, kx)))
    v_t = kvx[:, pair:].T.astype(_BF16)
    row = lax.broadcasted_iota(jnp.int32, (2 * w, half_g * w), 0)
    first_key = lax.broadcasted_iota(jnp.int32, (2 * w, pair), 0) == 0
    first_col = lax.broadcasted_iota(jnp.int32, (ATT_HEAD_DIM, 2 * w), 1) == 0
    no_prev = (row >= 1) & (row < w) & (n == 0)
    for j in range(ATT_ROWS // w):
        qs = slice(j * w, (j + 1) * w)
        ks = slice(j * w, (j + 2) * w)
        units = [(kv, par) for kv in range(ATT_KV_HEADS) for par in range(2)]
        scores = {}
        for kv in range(ATT_KV_HEADS):
            base = kv * ATT_GROUP * ATT_HEAD_DIM
            lhs = jnp.concatenate(
                [q_ref[qs, base + i * pair:base + (i + 1) * pair] for i in range(half_g)], axis=0)
            lhs = (lhs * (ATT_HEAD_DIM ** -0.5)).astype(_BF16)
            for par in range(2):
                keys = jnp.where(first_key, jnp.zeros((), _BF16), kmat[par][kv][ks])
                scores[kv, par] = _dot_nt(keys, lhs)
        probs = {}
        for kv, par in units:
            s = scores[kv, par] + bias_scr[kv, par]
            if j == 0:
                s = jnp.where(no_prev, _NEG_INF, s)
            m = jnp.max(s, axis=0, keepdims=True)
            p = jnp.exp(s - m)
            probs[kv, par] = (p.astype(_BF16), jnp.sum(p, axis=0, keepdims=True))
        outs = {}
        for kv, par in units:
            p, den = probs[kv, par]
            vals = v_t[kv * ATT_HEAD_DIM:(kv + 1) * ATT_HEAD_DIM, ks]
            vals = jnp.where(first_col, jnp.zeros((), _BF16), vals)
            outs[kv, par] = _dot(vals, p) / den
        for kv in range(ATT_KV_HEADS):
            base = kv * ATT_GROUP * ATT_HEAD_DIM
            for i in range(half_g):
                both = jnp.concatenate(
                    [outs[kv, par][:, i * w:(i + 1) * w] for par in range(2)], axis=0)
                o_ref[qs, base + i * pair:base + (i + 1) * pair] = both.T


def _attn_prompt(att, rel_bias, sinks):
    bsz, l, _ = att.shape
    rows = ATT_ROWS
    qi = np.arange(WINDOW)[None, :]
    si = np.arange(2 * WINDOW)[:, None]
    bidx = jnp.asarray(_t5_bucket_np(WINDOW + qi - si))
    kvcol = ATT_WIDTH // (2 * KV_WIDTH)
    blocks = rows // WINDOW
    smem = pl.BlockSpec(memory_space=pltpu.SMEM)
    return pl.pallas_call(
        _attn_prompt_kernel,
        grid=(bsz, l // rows),
        in_specs=[
            smem, smem,
            pl.BlockSpec((2 * WINDOW, WINDOW), lambda b, n: (0, 0)),
            pl.BlockSpec((None, rows, ATT_WIDTH), lambda b, n: (b, n, 0)),
            pl.BlockSpec((None, WINDOW, 2 * KV_WIDTH),
                         lambda b, n: (b, jnp.maximum(n * blocks - 1, 0), kvcol)),
            pl.BlockSpec((None, rows, 2 * KV_WIDTH), lambda b, n: (b, n, kvcol)),
        ],
        out_specs=pl.BlockSpec((None, rows, ATT_WIDTH), lambda b, n: (b, n, 0)),
        out_shape=jax.ShapeDtypeStruct((bsz, l, ATT_WIDTH), _F32),
        scratch_shapes=[pltpu.VMEM((ATT_KV_HEADS, 2, 2 * WINDOW, ATT_GROUP // 2 * WINDOW), _F32)],
        compiler_params=pltpu.CompilerParams(
            dimension_semantics=("arbitrary", "arbitrary"), vmem_limit_bytes=V7X_VMEM_LIMIT_BYTES),
        name="attn_prompt",
    )(rel_bias.reshape(-1), sinks, bidx, att, att, att)


def _attn_sample_kernel(relb_ref, sink_ref, bidx_ref, q_ref, kvn_ref, kc_ref, vc_ref,
                        o_ref, nk_ref, nv_ref, bias_scr, sink_scr):
    @pl.when(pl.program_id(0) == 0)
    def _():
        bidx = bidx_ref[...]
        row_g = lax.broadcasted_iota(jnp.int32, (ATT_GROUP * DEC_SEQ, V7X_LANES), 0) // DEC_SEQ
        for kv in range(ATT_KV_HEADS):
            bias = jnp.full(bidx.shape, _NEG_INF, _F32)
            sk = jnp.zeros(row_g.shape, _F32)
            for g in range(ATT_GROUP):
                h = kv * ATT_GROUP + g
                rows = lax.broadcasted_iota(jnp.int32, bidx.shape, 0) // DEC_SEQ == g
                bias = jnp.where(rows, _bias_from_buckets(bidx, relb_ref, h), bias)
                sk = jnp.where(row_g == g, sink_ref[h], sk)
            bias_scr[kv] = bias
            sink_scr[kv] = sk

    keep = WINDOW - DEC_SEQ
    units = [(b, kv) for b in range(SAMPLE_GROUP) for kv in range(ATT_KV_HEADS)]
    scores = {}
    for b in range(SAMPLE_GROUP):
        kvn = kvn_ref[b]
        nk_ref[b, :keep, :] = kc_ref[b, DEC_SEQ:, :]
        nv_ref[b, :keep, :] = vc_ref[b, DEC_SEQ:, :]
        nk_ref[b, keep:, :] = kvn[:DEC_SEQ, :KV_WIDTH]
        nv_ref[b, keep:, :] = kvn[:DEC_SEQ, KV_WIDTH:]
        k = jnp.concatenate([kc_ref[b], kvn[:, :KV_WIDTH]], axis=0).astype(_BF16)
        for kv in range(ATT_KV_HEADS):
            q = (q_ref[b, kv] * (ATT_HEAD_DIM ** -0.5)).astype(_BF16)
            scores[b, kv] = _dot_nt(q, k)
    probs = {}
    for b, kv in units:
        s = scores[b, kv] + bias_scr[kv]
        sink = sink_scr[kv][:, :1]
        m = jnp.maximum(jnp.max(s, axis=-1, keepdims=True), sink)
        p = jnp.exp(s - m)
        den = jnp.sum(p, axis=-1, keepdims=True) + jnp.exp(sink - m)
        probs[b, kv] = (p.astype(_BF16), den)
    for b in range(SAMPLE_GROUP):
        v = jnp.concatenate([vc_ref[b], kvn_ref[b][:, KV_WIDTH:]], axis=0).astype(_BF16)
        for kv in range(ATT_KV_HEADS):
            p, den = probs[b, kv]
            o = _dot(p, v) / den
            o_ref[b, kv] = o[:, kv * ATT_HEAD_DIM:(kv + 1) * ATT_HEAD_DIM]


def _attn_sample(q_r, kv_new, cache_k, cache_v, rel_bias, sinks):
    db = q_r.shape[0]
    rows = ATT_GROUP * DEC_SEQ
    new_rows = SAMPLE_KEYS_PAD - WINDOW
    t = (np.arange(rows) % DEC_SEQ)[:, None]
    c = np.arange(SAMPLE_KEYS_PAD)[None, :]
    dist = np.where(c < SAMPLE_KEYS, t + WINDOW - c, -1)
    bidx = jnp.asarray(_t5_bucket_np(dist))
    smem = pl.BlockSpec(memory_space=pltpu.SMEM)
    g = SAMPLE_GROUP
    return pl.pallas_call(
        _attn_sample_kernel,
        grid=(db // g,),
        in_specs=[
            smem, smem,
            pl.BlockSpec((rows, SAMPLE_KEYS_PAD), lambda i: (0, 0)),
            pl.BlockSpec((g, ATT_KV_HEADS, rows, KV_WIDTH), lambda i: (i, 0, 0, 0)),
            pl.BlockSpec((g, new_rows, 2 * KV_WIDTH), lambda i: (i, 0, 0)),
            pl.BlockSpec((g, WINDOW, KV_WIDTH), lambda i: (i, 0, 0)),
            pl.BlockSpec((g, WINDOW, KV_WIDTH), lambda i: (i, 0, 0)),
        ],
        out_specs=[
            pl.BlockSpec((g, ATT_KV_HEADS, rows, ATT_HEAD_DIM), lambda i: (i, 0, 0, 0)),
            pl.BlockSpec((g, WINDOW, KV_WIDTH), lambda i: (i, 0, 0)),
            pl.BlockSpec((g, WINDOW, KV_WIDTH), lambda i: (i, 0, 0)),
        ],
        out_shape=[
            jax.ShapeDtypeStruct((db, ATT_KV_HEADS, rows, ATT_HEAD_DIM), _F32),
            jax.ShapeDtypeStruct((db, WINDOW, KV_WIDTH), _F32),
            jax.ShapeDtypeStruct((db, WINDOW, KV_WIDTH), _F32),
        ],
        scratch_shapes=[
            pltpu.VMEM((ATT_KV_HEADS, rows, SAMPLE_KEYS_PAD), _F32),
            pltpu.VMEM((ATT_KV_HEADS, rows, V7X_LANES), _F32),
        ],
        compiler_params=pltpu.CompilerParams(dimension_semantics=("arbitrary",)),
        name="attn_sample",
    )(rel_bias.reshape(-1), sinks, bidx, q_r, kv_new, cache_k, cache_v)


def _lower_bound(lbg):
    mx = jnp.max(lbg, axis=0, keepdims=True)
    e = jnp.exp(lbg - mx)
    return e[0:1] / jnp.sum(e, axis=0, keepdims=True)


def _hgrn_gates(hg, lb):
    q = jax.nn.silu(hg[:, 0:HG_WIDTH])
    f = lb + (1.0 - lb) * jax.nn.sigmoid(hg[:, HG_WIDTH:2 * HG_WIDTH])
    v = hg[:, 2 * HG_WIDTH:3 * HG_WIDTH]
    gate = jax.nn.silu(hg[:, 3 * HG_WIDTH:4 * HG_WIDTH])
    return q, jnp.log(f), 1.0 - f, v, gate


def _hgrn_levels(c):
    return [c >> (i + 1) for i in range(int(math.log2(c)))]


def _hgrn_small_levels(c):
    return [h for h in _hgrn_levels(c) if 2 * h < V7X_SUBLANES]


def _hgrn_tables(c):
    lvl = np.full((c, c), -1, np.int32)
    j = np.arange(c)[None, :]
    t = np.arange(c)[:, None]
    mats = [j <= t]
    for li, h in enumerate(_hgrn_levels(c)):
        mid = (t // (2 * h)) * (2 * h) + h
        same = (t // (2 * h)) == (j // (2 * h))
        lvl[same & (t >= mid) & (j < mid)] = li
        if h in _hgrn_small_levels(c):
            mats.append(np.where(t >= mid, (j >= mid) & (j <= t), (j > t) & (j < mid)))
    return np.concatenate(mats, axis=0).astype(np.float32), np.concatenate([lvl, lvl], axis=1)


def _exact_rows_matmul(m, x):
    h1 = x.astype(_BF16)
    r1 = x - h1.astype(_F32)
    h2 = r1.astype(_BF16)
    h3 = (r1 - h2.astype(_F32)).astype(_BF16)
    return (_dot(m, h3) + _dot(m, h2)) + _dot(m, h1)


def _hgrn_level_exponents(sums, cb_ref, c):
    cb = sums[:c]
    small = _hgrn_small_levels(c)
    args = []
    for h in _hgrn_levels(c):
        if h in small:
            i = 1 + small.index(h)
            args.append(sums[i * c:(i + 1) * c])
        elif h % V7X_SUBLANES == 0:
            pieces = []
            for j in range(c // (2 * h)):
                mid = cb_ref[pl.ds(j * 2 * h + h - 1, 1), :]
                pieces.append(mid - cb[j * 2 * h:j * 2 * h + h])
                pieces.append(cb[j * 2 * h + h:(j + 1) * 2 * h] - mid)
            args.append(jnp.concatenate(pieces, axis=0))
        else:
            mid_cb = jnp.concatenate(
                [jnp.broadcast_to(cb_ref[pl.ds(j * 2 * h + h - 1, 1), :], (2 * h, cb.shape[1]))
                 for j in range(c // (2 * h))], axis=0)
            args.append(-jnp.abs(cb - mid_cb))
    return args


def _pair_blockdiag(x):
    left = lax.broadcasted_iota(jnp.int32, x.shape, 1) < HG_KEY
    zero = jnp.zeros_like(x)
    return jnp.concatenate([jnp.where(left, x, zero), jnp.where(left, zero, x)], axis=0)


def _hgrn_prompt_kernel(hg_ref, lbg_ref, g_ref, m_ref, lvl_ref, o_ref, s_ref, st_scr, cb_scr):
    i = pl.program_id(1)
    c = HG_CHUNK
    pw = 2 * HG_KEY

    @pl.when(i == 0)
    def _():
        st_scr[...] = jnp.zeros(st_scr.shape, _F32)

    lb = _lower_bound(lbg_ref[...])
    lvl = lvl_ref[...]
    rr = lax.broadcasted_iota(jnp.int32, (pw, pw), 0) < HG_VAL
    cc = lax.broadcasted_iota(jnp.int32, (pw, pw), 1) < HG_KEY
    same_head = rr == cc
    pairs = range(HG_HEADS // 2)
    q, k, v, gate, sums, args = {}, {}, {}, {}, {}, {}
    for p in pairs:
        ps = slice(p * pw, (p + 1) * pw)
        q[p] = jax.nn.silu(hg_ref[:, p * pw:(p + 1) * pw])
        f = lb[:, ps] + (1.0 - lb[:, ps]) * jax.nn.sigmoid(
            hg_ref[:, HG_WIDTH + p * pw:HG_WIDTH + (p + 1) * pw])
        v[p] = hg_ref[:, 2 * HG_WIDTH + p * pw:2 * HG_WIDTH + (p + 1) * pw]
        gate[p] = jax.nn.silu(hg_ref[:, 3 * HG_WIDTH + p * pw:3 * HG_WIDTH + (p + 1) * pw])
        k[p] = 1.0 - f
        sums[p] = _exact_rows_matmul(m_ref[...], jnp.log(f) * _LOG2_E)
    for p in pairs:
        cb_scr[p] = sums[p][:c]
        args[p] = _hgrn_level_exponents(sums[p], cb_scr.at[p], c)
    a = {p: jnp.zeros((c, pw), _F32) for p in pairs}
    for li in range(len(_hgrn_levels(c))):
        for p in pairs:
            e = jnp.exp2(args[p][li])
            pr = _dot_nt((q[p] * e).astype(_BF16), _pair_blockdiag((k[p] * e).astype(_BF16)))
            a[p] = jnp.where(lvl == li, pr, a[p])
    o = {}
    for p in pairs:
        cb = sums[p][:c]
        total = cb[c - 1:c, :]
        vb = v[p].astype(_BF16)
        st = st_scr[p]
        o[p] = (_dot_nt((q[p] * jnp.exp2(cb)).astype(_BF16), st.astype(_BF16))
                + _dot(a[p].astype(_BF16), _pair_blockdiag(vb)))
        upd = _dot_tn(vb, (k[p] * jnp.exp2(total - cb)).astype(_BF16))
        st_scr[p] = st * jnp.exp2(total) + jnp.where(same_head, upd, 0.0)
    for p in pairs:
        qk = q[p] * k[p]
        for hh in range(2):
            hs = slice(hh * HG_KEY, (hh + 1) * HG_KEY)
            gs = slice(p * pw + hh * HG_KEY, p * pw + (hh + 1) * HG_KEY)
            oh = o[p][:, hs] + jnp.sum(qk[:, hs], axis=-1, keepdims=True) * v[p][:, hs]
            oh = oh * lax.rsqrt(jnp.mean(oh * oh, axis=-1, keepdims=True) + RMS_EPS)
            o_ref[:, gs] = oh * g_ref[:, gs] * gate[p][:, hs]

    @pl.when(i == pl.num_programs(1) - 1)
    def _():
        for p in range(HG_HEADS // 2):
            st = st_scr[p]
            s_ref[2 * p] = st[:HG_VAL, :HG_KEY].T
            s_ref[2 * p + 1] = st[HG_VAL:, HG_KEY:].T


def _hgrn_prompt(hg, lb_gamma, hg_g):
    bsz, l, _ = hg.shape
    c = HG_CHUNK
    mats, lvl = _hgrn_tables(c)
    return pl.pallas_call(
        _hgrn_prompt_kernel,
        grid=(bsz, l // c),
        in_specs=[
            pl.BlockSpec((None, c, HG_SLAB), lambda b, i: (b, i, 0)),
            pl.BlockSpec(lb_gamma.shape, lambda b, i: (0, 0)),
            pl.BlockSpec((1, HG_WIDTH), lambda b, i: (0, 0)),
            pl.BlockSpec(mats.shape, lambda b, i: (0, 0)),
            pl.BlockSpec(lvl.shape, lambda b, i: (0, 0)),
        ],
        out_specs=[
            pl.BlockSpec((None, c, HG_WIDTH), lambda b, i: (b, i, 0)),
            pl.BlockSpec((None, HG_HEADS, HG_KEY, HG_VAL), lambda b, i: (b, 0, 0, 0)),
        ],
        out_shape=[
            jax.ShapeDtypeStruct((bsz, l, HG_WIDTH), _F32),
            jax.ShapeDtypeStruct((bsz, HG_HEADS, HG_KEY, HG_VAL), _F32),
        ],
        scratch_shapes=[
            pltpu.VMEM((HG_HEADS // 2, 2 * HG_VAL, 2 * HG_KEY), _F32),
            pltpu.VMEM((HG_HEADS // 2, c, 2 * HG_KEY), _F32),
        ],
        compiler_params=pltpu.CompilerParams(
            dimension_semantics=("arbitrary", "arbitrary"), vmem_limit_bytes=V7X_VMEM_LIMIT_BYTES),
        name="hgrn_prompt",
    )(hg, lb_gamma, hg_g, jnp.asarray(mats, _BF16), jnp.asarray(lvl))


def _hgrn_sample_kernel(hg_ref, lbg_ref, g_ref, mask_ref, s0_ref, o_ref, s_ref):
    nb = SAMPLE_GROUP
    lb = _lower_bound(lbg_ref[...])
    q, cb, k, v, gate = [], [], [], [], []
    for t in range(DEC_SEQ):
        qt, lf, kt, vt, gt = _hgrn_gates(hg_ref[t], lb)
        q.append(qt); k.append(kt); v.append(vt); gate.append(gt)
        cb.append(lf if t == 0 else cb[-1] + lf)
    total = cb[-1]
    mask = mask_ref[...]
    for h in range(HG_HEADS):
        hs = slice(h * HG_KEY, (h + 1) * HG_KEY)
        qdec = jnp.concatenate([q[t][:, hs] * jnp.exp(cb[t][:, hs]) for t in range(DEC_SEQ)], axis=0)
        kend = jnp.concatenate(
            [k[t][:, hs] * jnp.exp(total[:, hs] - cb[t][:, hs]) for t in range(DEC_SEQ)], axis=0)
        vall = jnp.concatenate([v[t][:, hs] for t in range(DEC_SEQ)], axis=0).astype(_BF16)
        qblk = (jnp.concatenate([qdec] * nb, axis=1) * mask).astype(_BF16)
        kblk = (jnp.concatenate([kend] * nb, axis=1) * mask).astype(_BF16)
        s0 = s0_ref[:, h]
        o_inter = _dot(qblk, s0.reshape(nb * HG_KEY, HG_VAL).astype(_BF16))
        upd = _dot_tn(kblk, vall)
        decay_t = jnp.exp(total[:, hs]).T
        for b in range(nb):
            s_ref[b, h] = s0[b] * decay_t[:, b:b + 1] + upd[b * HG_KEY:(b + 1) * HG_KEY]
        for t in range(DEC_SEQ):
            o = o_inter[t * nb:(t + 1) * nb]
            for s in range(t + 1):
                w = q[t][:, hs] * k[s][:, hs]
                if s < t:
                    w = w * jnp.exp(cb[t][:, hs] - cb[s][:, hs])
                o = o + jnp.sum(w, axis=-1, keepdims=True) * v[s][:, hs]
            o = o * lax.rsqrt(jnp.mean(o * o, axis=-1, keepdims=True) + RMS_EPS)
            o_ref[t, :, hs] = o * g_ref[:, hs] * gate[t][:, hs]


def _hgrn_sample(hg, s0, lb_gamma, hg_g):
    l, db, _ = hg.shape
    nb = SAMPLE_GROUP
    rb = np.arange(l * nb)[:, None] % nb
    cbk = np.arange(nb * HG_KEY)[None, :] // HG_KEY
    mask = jnp.asarray((rb == cbk).astype(np.float32))
    return pl.pallas_call(
        _hgrn_sample_kernel,
        grid=(db // nb,),
        in_specs=[
            pl.BlockSpec((l, nb, HG_SLAB), lambda i: (0, i, 0)),
            pl.BlockSpec(lb_gamma.shape, lambda i: (0, 0)),
            pl.BlockSpec((1, HG_WIDTH), lambda i: (0, 0)),
            pl.BlockSpec(mask.shape, lambda i: (0, 0)),
            pl.BlockSpec((nb, HG_HEADS, HG_KEY, HG_VAL), lambda i: (i, 0, 0, 0)),
        ],
        out_specs=[
            pl.BlockSpec((l, nb, HG_WIDTH), lambda i: (0, i, 0)),
            pl.BlockSpec((nb, HG_HEADS, HG_KEY, HG_VAL), lambda i: (i, 0, 0, 0)),
        ],
        out_shape=[
            jax.ShapeDtypeStruct((l, db, HG_WIDTH), _F32),
            jax.ShapeDtypeStruct((db, HG_HEADS, HG_KEY, HG_VAL), _F32),
        ],
        compiler_params=pltpu.CompilerParams(
            dimension_semantics=("arbitrary",), vmem_limit_bytes=V7X_VMEM_LIMIT_BYTES),
        name="hgrn_sample",
    )(hg, lb_gamma, hg_g, mask, s0)


def _ffn_kernel(*refs, shift, blocks_per_seq, has_prev):
    if has_prev:
        (x_ref, oa_ref, oh_ref, ag_ref, wo_ref, n2_ref, wfi_ref, cw_ref, cb_ref, wfo_ref, fg_ref,
         prev_ref, y_ref, conv_ref, buf, hid) = refs
    else:
        (x_ref, oa_ref, oh_ref, ag_ref, wo_ref, n2_ref, wfi_ref, cw_ref, cb_ref, wfo_ref, fg_ref,
         y_ref, conv_ref, buf, hid) = refs
    tm = x_ref.shape[0]
    pad = buf.shape[0] - tm
    i = pl.program_id(0)

    oa = _rms(oa_ref[...], ag_ref[...])
    mix = jnp.concatenate([oa.astype(_BF16), oh_ref[...].astype(_BF16)], axis=-1)
    x1 = x_ref[...] + _dot(mix, wo_ref[...])
    h2 = _rms(x1, n2_ref[...]).astype(_BF16)

    if has_prev:
        buf[:pad, :] = prev_ref[...]
    else:
        @pl.when(i % blocks_per_seq == 0)
        def _():
            buf[:pad, :] = jnp.zeros((pad, D_FF), _F32)

        @pl.when(i % blocks_per_seq != 0)
        def _():
            buf[:pad, :] = buf[tm:tm + pad, :]

    for c in range(D_FF // FFN_COLS):
        cs = slice(c * FFN_COLS, (c + 1) * FFN_COLS)
        a = _dot(h2, wfi_ref[:, cs])
        gate = _dot(h2, wfi_ref[:, D_FF + c * FFN_COLS:D_FF + (c + 1) * FFN_COLS])
        buf[pad:, cs] = a
        ac = cb_ref[:, cs] + buf[pad - 2 * shift:pad - 2 * shift + tm, cs] * cw_ref[0:1, cs]
        ac = ac + buf[pad - shift:pad - shift + tm, cs] * cw_ref[1:2, cs]
        ac = ac + a * cw_ref[2:3, cs]
        hid[:, cs] = (jax.nn.silu(ac) * gate).astype(_BF16)
    y_ref[...] = _rms(x1 + _dot(hid[...], wfo_ref[...]), fg_ref[...])
    conv_ref[...] = buf[pad + tm - 2 * shift:, :]


def _ffn(x, oa, oh, attn_g, w_o, n2, wf_in, conv_w, conv_b, wf_out, final_g, *, shift,
         rows_per_seq, prev=None):
    t = x.shape[0]
    tm = FFN_ROWS
    nblk = t // tm
    blocks_per_seq = rows_per_seq // tm
    nseq = nblk // blocks_per_seq
    pad = max(V7X_SUBLANES, 2 * shift)
    tail = 2 * shift
    const = functools.partial(pl.BlockSpec, pipeline_mode=pl.Buffered(1))
    in_specs = [
        pl.BlockSpec((tm, D_MODEL), lambda i: (i, 0)),
        pl.BlockSpec((tm, ATT_WIDTH), lambda i: (i, 0)),
        pl.BlockSpec((tm, HG_WIDTH), lambda i: (i, 0)),
        const((1, ATT_WIDTH), lambda i: (0, 0)),
        const((ATT_WIDTH + HG_WIDTH, D_MODEL), lambda i: (0, 0)),
        const((1, D_MODEL), lambda i: (0, 0)),
        const((D_MODEL, 2 * D_FF), lambda i: (0, 0)),
        const((CONV_W, D_FF), lambda i: (0, 0)),
        const((1, D_FF), lambda i: (0, 0)),
        const((D_FF, D_MODEL), lambda i: (0, 0)),
        const((1, D_MODEL), lambda i: (0, 0)),
    ]
    args = [x, oa, oh, attn_g, w_o, n2, wf_in, conv_w, conv_b, wf_out, final_g]
    if prev is not None:
        in_specs.append(const((pad, D_FF), lambda i: (0, 0)))
        args.append(prev)
    return pl.pallas_call(
        functools.partial(_ffn_kernel, shift=shift, blocks_per_seq=blocks_per_seq,
                          has_prev=prev is not None),
        grid=(nblk,),
        in_specs=in_specs,
        out_specs=[
            pl.BlockSpec((tm, D_MODEL), lambda i: (i, 0)),
            pl.BlockSpec((None, tail, D_FF), lambda i: (i // blocks_per_seq, 0, 0)),
        ],
        out_shape=[
            jax.ShapeDtypeStruct((t, D_MODEL), _F32),
            jax.ShapeDtypeStruct((nseq, tail, D_FF), _F32),
        ],
        scratch_shapes=[pltpu.VMEM((pad + tm, D_FF), _F32), pltpu.VMEM((tm, D_FF), _BF16)],
        compiler_params=pltpu.CompilerParams(
            dimension_semantics=("arbitrary",), vmem_limit_bytes=V7X_VMEM_LIMIT_BYTES),
        name="ffn",
    )(*args)


def kernel(x_prompt, x_sample, cache_k_win, cache_v_win, state_hgrn, state_conv, norm1_g, w_in,
           attn_sinks, rel_bias, lb_gamma, attn_out_g, hg_out_g, w_out, norm2_g, w_ffn_in, conv_w,
           conv_b, w_ffn_out, final_g):
    assert DEPTH == 1
    n1 = norm1_g[0][None]
    n2 = norm2_g[0][None]
    ag = attn_out_g[0][None]
    hgg = hg_out_g[0][None]
    fg = final_g[None]
    cb = conv_b[0][None]
    cw = conv_w[0]
    sinks = attn_sinks[0]
    w_in_b = w_in[0].astype(_BF16)
    w_o_b = w_out[0].astype(_BF16)
    wf_in_b = w_ffn_in[0].astype(_BF16)
    wf_out_b = w_ffn_out[0].astype(_BF16)
    ffn_w = (ag, w_o_b, n2, wf_in_b, cw, cb, wf_out_b, fg)

    xp = x_prompt.reshape(BATCH * SEQ, D_MODEL)
    att_p, hg_p = _proj(xp, n1, w_in_b)
    att_p = att_p.reshape(BATCH, SEQ, ATT_SLAB)
    oa_p = _attn_prompt(att_p, rel_bias, sinks)
    oh_p, s_p = _hgrn_prompt(hg_p.reshape(BATCH, SEQ, HG_SLAB), lb_gamma, hgg)
    y_p, conv_p = _ffn(xp, oa_p.reshape(BATCH * SEQ, ATT_WIDTH), oh_p.reshape(BATCH * SEQ, HG_WIDTH),
                       *ffn_w, shift=1, rows_per_seq=SEQ)
    kv_tail = att_p[:, SEQ - WINDOW:, ATT_WIDTH:]
    new_k_p = kv_tail[:, :, :KV_WIDTH].reshape(1, BATCH, WINDOW, ATT_KV_HEADS, ATT_HEAD_DIM)
    new_v_p = kv_tail[:, :, KV_WIDTH:].reshape(1, BATCH, WINDOW, ATT_KV_HEADS, ATT_HEAD_DIM)

    rows_s = DEC_SEQ * DEC_BATCH
    xs = x_sample.transpose(1, 0, 2).reshape(rows_s, D_MODEL)
    att_s, hg_s = _proj(xs, n1, w_in_b)
    att_s = att_s.reshape(DEC_SEQ, DEC_BATCH, ATT_SLAB)
    q_r = att_s[:, :, :ATT_WIDTH].reshape(DEC_SEQ, DEC_BATCH, ATT_KV_HEADS, ATT_GROUP, ATT_HEAD_DIM)
    q_r = q_r.transpose(1, 2, 3, 0, 4).reshape(DEC_BATCH, ATT_KV_HEADS, ATT_GROUP * DEC_SEQ, ATT_HEAD_DIM)
    q_zero = jnp.zeros_like(q_r[:, 0])
    q_r = jnp.stack([jnp.concatenate([q_r[:, 0], q_zero], axis=-1),
                     jnp.concatenate([q_zero, q_r[:, 1]], axis=-1)], axis=1)
    kv_new = att_s[:, :, ATT_WIDTH:].transpose(1, 0, 2)
    kv_new = jnp.pad(kv_new, ((0, 0), (0, SAMPLE_KEYS_PAD - SAMPLE_KEYS), (0, 0)))
    o16, new_k_s, new_v_s = _attn_sample(
        q_r, kv_new, cache_k_win[0].reshape(DEC_BATCH, WINDOW, KV_WIDTH),
        cache_v_win[0].reshape(DEC_BATCH, WINDOW, KV_WIDTH), rel_bias, sinks)
    oa_s = o16.reshape(DEC_BATCH, ATT_KV_HEADS, ATT_GROUP, DEC_SEQ, ATT_HEAD_DIM)
    oa_s = oa_s.transpose(3, 0, 1, 2, 4).reshape(rows_s, ATT_WIDTH)
    oh_s, s_s = _hgrn_sample(hg_s.reshape(DEC_SEQ, DEC_BATCH, HG_SLAB), state_hgrn[0], lb_gamma, hgg)
    prev = state_conv[0].transpose(1, 0, 2).reshape((CONV_W - 1) * DEC_BATCH, D_FF)
    y_s, conv_s = _ffn(xs, oa_s, oh_s.reshape(rows_s, HG_WIDTH), *ffn_w, shift=DEC_BATCH,
                       rows_per_seq=rows_s, prev=prev)
    y_s = y_s.reshape(DEC_SEQ, DEC_BATCH, D_MODEL).transpose(1, 0, 2)
    conv_s = conv_s.reshape(CONV_W - 1, DEC_BATCH, D_FF).transpose(1, 0, 2)

    cache_shape = (1, DEC_BATCH, WINDOW, ATT_KV_HEADS, ATT_HEAD_DIM)
    return (y_p.reshape(BATCH, SEQ, D_MODEL), y_s, new_k_p, new_v_p, s_p[None], conv_p[None],
            new_k_s.reshape(cache_shape), new_v_s.reshape(cache_shape), s_s[None], conv_s[None])
```

```python
import functools
import math

import jax
import jax.numpy as jnp
import numpy as np
from jax import lax
from jax.experimental import pallas as pl
from jax.experimental.pallas import tpu as pltpu

D_MODEL = 1024
BATCH = 2
SEQ = 8192
DEPTH = 1
DEC_BATCH = 128
DEC_SEQ = 4
ATT_HEAD_DIM = 64
ATT_HEADS = 8
ATT_KV_HEADS = 2
ATT_GROUP = 4
ATT_WIDTH = 512
KV_WIDTH = ATT_KV_HEADS * ATT_HEAD_DIM
WINDOW = 128
NUM_BUCKETS = 32
MAX_DISTANCE = 128
HG_KEY = 128
HG_VAL = 128
HG_HEADS = 4
HG_WIDTH = 512
D_FF = 2816
CONV_W = 3
RMS_EPS = 1e-6
ATT_SLAB = ATT_WIDTH + 2 * KV_WIDTH
HG_SLAB = 4 * HG_WIDTH
PROJ_WIDTH = ATT_SLAB + HG_SLAB

V7X_SUBLANES = 8
V7X_LANES = 128
V7X_MXU_DIM = 256
V7X_VMEM_LIMIT_BYTES = 56 * 1024 * 1024

PROJ_ROWS = 512
LAYER_ROWS = 256
ATT_ROWS = 512
FFN_ROWS = 512
FFN_COLS = V7X_MXU_DIM
HG_CHUNK = 128
SAMPLE_GROUP = 8
SAMPLE_KEYS = WINDOW + DEC_SEQ
SAMPLE_KEYS_PAD = SAMPLE_KEYS + (-SAMPLE_KEYS) % V7X_SUBLANES

_F32 = jnp.float32
_BF16 = jnp.bfloat16
_NEG_INF = float("-inf")
_LOG2_E = 1.0 / math.log(2.0)


def _dot(a, b):
    return jnp.dot(a, b, preferred_element_type=_F32)


def _dot_nt(a, b):
    return lax.dot_general(a, b, (((1,), (1,)), ((), ())), preferred_element_type=_F32)


def _dot_tn(a, b):
    return lax.dot_general(a, b, (((0,), (0,)), ((), ())), preferred_element_type=_F32)


def _rms(x, g):
    return x * lax.rsqrt(jnp.mean(x * x, axis=-1, keepdims=True) + RMS_EPS) * g


def _t5_bucket_np(dist):
    dist = np.asarray(dist)
    n = np.maximum(dist, 0)
    max_exact = NUM_BUCKETS // 2
    nf = np.maximum(n, 1).astype(np.float64)
    large = max_exact + (np.log(nf / max_exact) / math.log(MAX_DISTANCE / max_exact)
                         * (NUM_BUCKETS - max_exact)).astype(np.int32)
    large = np.minimum(large, NUM_BUCKETS - 1)
    bucket = np.where(n < max_exact, n, large)
    return np.where((dist >= 0) & (dist < WINDOW), bucket, -1).astype(np.int32)


def _bias_from_buckets(bidx, relb_ref, head):
    def body(b, acc):
        return jnp.where(bidx == b, relb_ref[b * ATT_HEADS + head], acc)
    return lax.fori_loop(0, NUM_BUCKETS, body, jnp.full(bidx.shape, _NEG_INF, _F32))


def _proj_kernel(x_ref, g_ref, w_ref, att_ref, hg_ref):
    h = _rms(x_ref[...], g_ref[...]).astype(_BF16)
    res = _dot(h, w_ref[...])
    att_ref[...] = res[:, :ATT_SLAB]
    hg_ref[...] = res[:, ATT_SLAB:]


def _proj(x, g, w):
    t = x.shape[0]
    tm = PROJ_ROWS
    return pl.pallas_call(
        _proj_kernel,
        grid=(t // tm,),
        in_specs=[
            pl.BlockSpec((tm, D_MODEL), lambda i: (i, 0)),
            pl.BlockSpec((1, D_MODEL), lambda i: (0, 0)),
            pl.BlockSpec((D_MODEL, PROJ_WIDTH), lambda i: (0, 0)),
        ],
        out_specs=[
            pl.BlockSpec((tm, ATT_SLAB), lambda i: (i, 0)),
            pl.BlockSpec((tm, HG_SLAB), lambda i: (i, 0)),
        ],
        out_shape=[
            jax.ShapeDtypeStruct((t, ATT_SLAB), _F32),
            jax.ShapeDtypeStruct((t, HG_SLAB), _F32),
        ],
        compiler_params=pltpu.CompilerParams(
            dimension_semantics=("arbitrary",), vmem_limit_bytes=V7X_VMEM_LIMIT_BYTES),
        name="proj",
    )(x, g, w)


def _attn_prompt_kernel(relb_ref, sink_ref, bidx_ref, q_ref, kvp_ref, kvc_ref, o_ref, bias_scr):
    b = pl.program_id(0)
    n = pl.program_id(1)
    w = WINDOW
    pair = 2 * ATT_HEAD_DIM
    half_g = ATT_GROUP // 2

    @pl.when((b == 0) & (n == 0))
    def _():
        bidx = bidx_ref[...]
        key0 = lax.broadcasted_iota(jnp.int32, bidx.shape, 0) == 0
        for kv in range(ATT_KV_HEADS):
            for par in range(2):
                for i in range(half_g):
                    h = kv * ATT_GROUP + 2 * i + par
                    tab = _bias_from_buckets(bidx, relb_ref, h)
                    bias_scr[kv, par, :, i * w:(i + 1) * w] = jnp.where(key0, sink_ref[h], tab)

    kvx = jnp.concatenate([kvp_ref[...], kvc_ref[...]], axis=0)
    kx = kvx[:, :pair]
    kr = pltpu.roll(kx, ATT_HEAD_DIM, 1).astype(_BF16)
    kx = kx.astype(_BF16)
    lo = lax.broadcasted_iota(jnp.int32, kx.shape, 1) < ATT_HEAD_DIM
    zero = jnp.zeros_like(kx)
    kmat = ((jnp.where(lo, kx, zero), jnp.where(lo, kr, zero)),
            (jnp.where(lo, zero, kr), jnp.where(lo, zero, kx)))
    v_t = kvx[:, pair:].T.astype(_BF16)
    row = lax.broadcasted_iota(jnp.int32, (2 * w, half_g * w), 0)
    first_key = lax.broadcasted_iota(jnp.int32, (2 * w, pair), 0) == 0
    first_col = lax.broadcasted_iota(jnp.int32, (ATT_HEAD_DIM, 2 * w), 1) == 0
    no_prev = (row >= 1) & (row < w) & (n == 0)
    for j in range(ATT_ROWS // w):
        qs = slice(j * w, (j + 1) * w)
        ks = slice(j * w, (j + 2) * w)
        units = [(kv, par) for kv in range(ATT_KV_HEADS) for par in range(2)]
        scores = {}
        for kv in range(ATT_KV_HEADS):
            base = kv * ATT_GROUP * ATT_HEAD_DIM
            lhs = jnp.concatenate(
                [q_ref[qs, base + i * pair:base + (i + 1) * pair] for i in range(half_g)], axis=0)
            lhs = (lhs * (ATT_HEAD_DIM ** -0.5)).astype(_BF16)
            for par in range(2):
                keys = jnp.where(first_key, jnp.zeros((), _BF16), kmat[par][kv][ks])
                scores[kv, par] = _dot_nt(keys, lhs)
        probs = {}
        for kv, par in units:
            s = scores[kv, par] + bias_scr[kv, par]
            if j == 0:
                s = jnp.where(no_prev, _NEG_INF, s)
            m = jnp.max(s, axis=0, keepdims=True)
            p = jnp.exp(s - m)
            probs[kv, par] = (p.astype(_BF16), jnp.sum(p, axis=0, keepdims=True))
        outs = {}
        for kv, par in units:
            p, den = probs[kv, par]
            vals = v_t[kv * ATT_HEAD_DIM:(kv + 1) * ATT_HEAD_DIM, ks]
            vals = jnp.where(first_col, jnp.zeros((), _BF16), vals)
            outs[kv, par] = _dot(vals, p) / den
        for kv in range(ATT_KV_HEADS):
            base = kv * ATT_GROUP * ATT_HEAD_DIM
            for i in range(half_g):
                both = jnp.concatenate(
                    [outs[kv, par][:, i * w:(i + 1) * w] for par in range(2)], axis=0)
                o_ref[qs, base + i * pair:base + (i + 1) * pair] = both.T


def _attn_prompt(att, rel_bias, sinks):
    bsz, l, _ = att.shape
    rows = ATT_ROWS
    qi = np.arange(WINDOW)[None, :]
    si = np.arange(2 * WINDOW)[:, None]
    bidx = jnp.asarray(_t5_bucket_np(WINDOW + qi - si))
    kvcol = ATT_WIDTH // (2 * KV_WIDTH)
    blocks = rows // WINDOW
    smem = pl.BlockSpec(memory_space=pltpu.SMEM)
    return pl.pallas_call(
        _attn_prompt_kernel,
        grid=(bsz, l // rows),
        in_specs=[
            smem, smem,
            pl.BlockSpec((2 * WINDOW, WINDOW), lambda b, n: (0, 0)),
            pl.BlockSpec((None, rows, ATT_WIDTH), lambda b, n: (b, n, 0)),
            pl.BlockSpec((None, WINDOW, 2 * KV_WIDTH),
                         lambda b, n: (b, jnp.maximum(n * blocks - 1, 0), kvcol)),
            pl.BlockSpec((None, rows, 2 * KV_WIDTH), lambda b, n: (b, n, kvcol)),
        ],
        out_specs=pl.BlockSpec((None, rows, ATT_WIDTH), lambda b, n: (b, n, 0)),
        out_shape=jax.ShapeDtypeStruct((bsz, l, ATT_WIDTH), _F32),
        scratch_shapes=[pltpu.VMEM((ATT_KV_HEADS, 2, 2 * WINDOW, ATT_GROUP // 2 * WINDOW), _F32)],
        compiler_params=pltpu.CompilerParams(
            dimension_semantics=("arbitrary", "arbitrary"), vmem_limit_bytes=V7X_VMEM_LIMIT_BYTES),
        name="attn_prompt",
    )(rel_bias.reshape(-1), sinks, bidx, att, att, att)


def _attn_sample_kernel(relb_ref, sink_ref, bidx_ref, q_ref, kvn_ref, kc_ref, vc_ref,
                        o_ref, nk_ref, nv_ref, bias_scr, sink_scr):
    @pl.when(pl.program_id(0) == 0)
    def _():
        bidx = bidx_ref[...]
        row_g = lax.broadcasted_iota(jnp.int32, (ATT_GROUP * DEC_SEQ, V7X_LANES), 0) // DEC_SEQ
        for kv in range(ATT_KV_HEADS):
            bias = jnp.full(bidx.shape, _NEG_INF, _F32)
            sk = jnp.zeros(row_g.shape, _F32)
            for g in range(ATT_GROUP):
                h = kv * ATT_GROUP + g
                rows = lax.broadcasted_iota(jnp.int32, bidx.shape, 0) // DEC_SEQ == g
                bias = jnp.where(rows, _bias_from_buckets(bidx, relb_ref, h), bias)
                sk = jnp.where(row_g == g, sink_ref[h], sk)
            bias_scr[kv] = bias
            sink_scr[kv] = sk

    keep = WINDOW - DEC_SEQ
    units = [(b, kv) for b in range(SAMPLE_GROUP) for kv in range(ATT_KV_HEADS)]
    scores = {}
    for b in range(SAMPLE_GROUP):
        kvn = kvn_ref[b]
        nk_ref[b, :keep, :] = kc_ref[b, DEC_SEQ:, :]
        nv_ref[b, :keep, :] = vc_ref[b, DEC_SEQ:, :]
        nk_ref[b, keep:, :] = kvn[:DEC_SEQ, :KV_WIDTH]
        nv_ref[b, keep:, :] = kvn[:DEC_SEQ, KV_WIDTH:]
        k = jnp.concatenate([kc_ref[b], kvn[:, :KV_WIDTH]], axis=0).astype(_BF16)
        for kv in range(ATT_KV_HEADS):
            q = (q_ref[b, kv] * (ATT_HEAD_DIM ** -0.5)).astype(_BF16)
            scores[b, kv] = _dot_nt(q, k)
    probs = {}
    for b, kv in units:
        s = scores[b, kv] + bias_scr[kv]
        sink = sink_scr[kv][:, :1]
        m = jnp.maximum(jnp.max(s, axis=-1, keepdims=True), sink)
        p = jnp.exp(s - m)
        den = jnp.sum(p, axis=-1, keepdims=True) + jnp.exp(sink - m)
        probs[b, kv] = (p.astype(_BF16), den)
    for b in range(SAMPLE_GROUP):
        v = jnp.concatenate([vc_ref[b], kvn_ref[b][:, KV_WIDTH:]], axis=0).astype(_BF16)
        for kv in range(ATT_KV_HEADS):
            p, den = probs[b, kv]
            o = _dot(p, v) / den
            o_ref[b, kv] = o[:, kv * ATT_HEAD_DIM:(kv + 1) * ATT_HEAD_DIM]


def _attn_sample(q_r, kv_new, cache_k, cache_v, rel_bias, sinks):
    db = q_r.shape[0]
    rows = ATT_GROUP * DEC_SEQ
    new_rows = SAMPLE_KEYS_PAD - WINDOW
    t = (np.arange(rows) % DEC_SEQ)[:, None]
    c = np.arange(SAMPLE_KEYS_PAD)[None, :]
    dist = np.where(c < SAMPLE_KEYS, t + WINDOW - c, -1)
    bidx = jnp.asarray(_t5_bucket_np(dist))
    smem = pl.BlockSpec(memory_space=pltpu.SMEM)
    g = SAMPLE_GROUP
    return pl.pallas_call(
        _attn_sample_kernel,
        grid=(db // g,),
        in_specs=[
            smem, smem,
            pl.BlockSpec((rows, SAMPLE_KEYS_PAD), lambda i: (0, 0)),
            pl.BlockSpec((g, ATT_KV_HEADS, rows, KV_WIDTH), lambda i: (i, 0, 0, 0)),
            pl.BlockSpec((g, new_rows, 2 * KV_WIDTH), lambda i: (i, 0, 0)),
            pl.BlockSpec((g, WINDOW, KV_WIDTH), lambda i: (i, 0, 0)),
            pl.BlockSpec((g, WINDOW, KV_WIDTH), lambda i: (i, 0, 0)),
        ],
        out_specs=[
            pl.BlockSpec((g, ATT_KV_HEADS, rows, ATT_HEAD_DIM), lambda i: (i, 0, 0, 0)),
            pl.BlockSpec((g, WINDOW, KV_WIDTH), lambda i: (i, 0, 0)),
            pl.BlockSpec((g, WINDOW, KV_WIDTH), lambda i: (i, 0, 0)),
        ],
        out_shape=[
            jax.ShapeDtypeStruct((db, ATT_KV_HEADS, rows, ATT_HEAD_DIM), _F32),
            jax.ShapeDtypeStruct((db, WINDOW, KV_WIDTH), _F32),
            jax.ShapeDtypeStruct((db, WINDOW, KV_WIDTH), _F32),
        ],
        scratch_shapes=[
            pltpu.VMEM((ATT_KV_HEADS, rows, SAMPLE_KEYS_PAD), _F32),
            pltpu.VMEM((ATT_KV_HEADS, rows, V7X_LANES), _F32),
        ],
        compiler_params=pltpu.CompilerParams(dimension_semantics=("arbitrary",)),
        name="attn_sample",
    )(rel_bias.reshape(-1), sinks, bidx, q_r, kv_new, cache_k, cache_v)


def _lower_bound(lbg):
    mx = jnp.max(lbg, axis=0, keepdims=True)
    e = jnp.exp(lbg - mx)
    return e[0:1] / jnp.sum(e, axis=0, keepdims=True)


def _hgrn_gates(hg, lb):
    q = jax.nn.silu(hg[:, 0:HG_WIDTH])
    f = lb + (1.0 - lb) * jax.nn.sigmoid(hg[:, HG_WIDTH:2 * HG_WIDTH])
    v = hg[:, 2 * HG_WIDTH:3 * HG_WIDTH]
    gate = jax.nn.silu(hg[:, 3 * HG_WIDTH:4 * HG_WIDTH])
    return q, jnp.log(f), 1.0 - f, v, gate


def _hgrn_levels(c):
    return [c >> (i + 1) for i in range(int(math.log2(c)))]


def _hgrn_small_levels(c):
    return [h for h in _hgrn_levels(c) if 2 * h < V7X_SUBLANES]


def _hgrn_tables(c):
    lvl = np.full((c, c), -1, np.int32)
    j = np.arange(c)[None, :]
    t = np.arange(c)[:, None]
    mats = [j <= t]
    for li, h in enumerate(_hgrn_levels(c)):
        mid = (t // (2 * h)) * (2 * h) + h
        same = (t // (2 * h)) == (j // (2 * h))
        lvl[same & (t >= mid) & (j < mid)] = li
        if h in _hgrn_small_levels(c):
            mats.append(np.where(t >= mid, (j >= mid) & (j <= t), (j > t) & (j < mid)))
    return np.concatenate(mats, axis=0).astype(np.float32), np.concatenate([lvl, lvl], axis=1)


def _exact_rows_matmul(m, x):
    h1 = x.astype(_BF16)
    r1 = x - h1.astype(_F32)
    h2 = r1.astype(_BF16)
    h3 = (r1 - h2.astype(_F32)).astype(_BF16)
    return (_dot(m, h3) + _dot(m, h2)) + _dot(m, h1)


def _hgrn_level_exponents(sums, cb_ref, c):
    cb = sums[:c]
    small = _hgrn_small_levels(c)
    args = []
    for h in _hgrn_levels(c):
        if h in small:
            i = 1 + small.index(h)
            args.append(sums[i * c:(i + 1) * c])
        elif h % V7X_SUBLANES == 0:
            pieces = []
            for j in range(c // (2 * h)):
                mid = cb_ref[pl.ds(j * 2 * h + h - 1, 1), :]
                pieces.append(mid - cb[j * 2 * h:j * 2 * h + h])
                pieces.append(cb[j * 2 * h + h:(j + 1) * 2 * h] - mid)
            args.append(jnp.concatenate(pieces, axis=0))
        else:
            mid_cb = jnp.concatenate(
                [jnp.broadcast_to(cb_ref[pl.ds(j * 2 * h + h - 1, 1), :], (2 * h, cb.shape[1]))
                 for j in range(c // (2 * h))], axis=0)
            args.append(-jnp.abs(cb - mid_cb))
    return args


def _pair_blockdiag(x):
    left = lax.broadcasted_iota(jnp.int32, x.shape, 1) < HG_KEY
    zero = jnp.zeros_like(x)
    return jnp.concatenate([jnp.where(left, x, zero), jnp.where(left, zero, x)], axis=0)


def _hgrn_prompt_kernel(hg_ref, lbg_ref, g_ref, m_ref, lvl_ref, o_ref, s_ref, st_scr, cb_scr):
    i = pl.program_id(1)
    c = HG_CHUNK
    pw = 2 * HG_KEY

    @pl.when(i == 0)
    def _():
        st_scr[...] = jnp.zeros(st_scr.shape, _F32)

    lb = _lower_bound(lbg_ref[...])
    lvl = lvl_ref[...]
    rr = lax.broadcasted_iota(jnp.int32, (pw, pw), 0) < HG_VAL
    cc = lax.broadcasted_iota(jnp.int32, (pw, pw), 1) < HG_KEY
    same_head = rr == cc
    pairs = range(HG_HEADS // 2)
    q, k, v, gate, sums, args = {}, {}, {}, {}, {}, {}
    for p in pairs:
        ps = slice(p * pw, (p + 1) * pw)
        q[p] = jax.nn.silu(hg_ref[:, p * pw:(p + 1) * pw])
        f = lb[:, ps] + (1.0 - lb[:, ps]) * jax.nn.sigmoid(
            hg_ref[:, HG_WIDTH + p * pw:HG_WIDTH + (p + 1) * pw])
        v[p] = hg_ref[:, 2 * HG_WIDTH + p * pw:2 * HG_WIDTH + (p + 1) * pw]
        gate[p] = jax.nn.silu(hg_ref[:, 3 * HG_WIDTH + p * pw:3 * HG_WIDTH + (p + 1) * pw])
        k[p] = 1.0 - f
        sums[p] = _exact_rows_matmul(m_ref[...], jnp.log(f) * _LOG2_E)
    for p in pairs:
        cb_scr[p] = sums[p][:c]
        args[p] = _hgrn_level_exponents(sums[p], cb_scr.at[p], c)
    a = {p: jnp.zeros((c, pw), _F32) for p in pairs}
    for li in range(len(_hgrn_levels(c))):
        for p in pairs:
            e = jnp.exp2(args[p][li])
            pr = _dot_nt((q[p] * e).astype(_BF16), _pair_blockdiag((k[p] * e).astype(_BF16)))
            a[p] = jnp.where(lvl == li, pr, a[p])
    o = {}
    for p in pairs:
        cb = sums[p][:c]
        total = cb[c - 1:c, :]
        vb = v[p].astype(_BF16)
        st = st_scr[p]
        o[p] = (_dot_nt((q[p] * jnp.exp2(cb)).astype(_BF16), st.astype(_BF16))
                + _dot(a[p].astype(_BF16), _pair_blockdiag(vb)))
        upd = _dot_tn(vb, (k[p] * jnp.exp2(total - cb)).astype(_BF16))
        st_scr[p] = st * jnp.exp2(total) + jnp.where(same_head, upd, 0.0)
    for p in pairs:
        qk = q[p] * k[p]
        for hh in range(2):
            hs = slice(hh * HG_KEY, (hh + 1) * HG_KEY)
            gs = slice(p * pw + hh * HG_KEY, p * pw + (hh + 1) * HG_KEY)
            oh = o[p][:, hs] + jnp.sum(qk[:, hs], axis=-1, keepdims=True) * v[p][:, hs]
            oh = oh * lax.rsqrt(jnp.mean(oh * oh, axis=-1, keepdims=True) + RMS_EPS)
            o_ref[:, gs] = oh * g_ref[:, gs] * gate[p][:, hs]

    @pl.when(i == pl.num_programs(1) - 1)
    def _():
        for p in range(HG_HEADS // 2):
            st = st_scr[p]
            s_ref[2 * p] = st[:HG_VAL, :HG_KEY].T
            s_ref[2 * p + 1] = st[HG_VAL:, HG_KEY:].T


def _hgrn_prompt(hg, lb_gamma, hg_g):
    bsz, l, _ = hg.shape
    c = HG_CHUNK
    mats, lvl = _hgrn_tables(c)
    return pl.pallas_call(
        _hgrn_prompt_kernel,
        grid=(bsz, l // c),
        in_specs=[
            pl.BlockSpec((None, c, HG_SLAB), lambda b, i: (b, i, 0)),
            pl.BlockSpec(lb_gamma.shape, lambda b, i: (0, 0)),
            pl.BlockSpec((1, HG_WIDTH), lambda b, i: (0, 0)),
            pl.BlockSpec(mats.shape, lambda b, i: (0, 0)),
            pl.BlockSpec(lvl.shape, lambda b, i: (0, 0)),
        ],
        out_specs=[
            pl.BlockSpec((None, c, HG_WIDTH), lambda b, i: (b, i, 0)),
            pl.BlockSpec((None, HG_HEADS, HG_KEY, HG_VAL), lambda b, i: (b, 0, 0, 0)),
        ],
        out_shape=[
            jax.ShapeDtypeStruct((bsz, l, HG_WIDTH), _F32),
            jax.ShapeDtypeStruct((bsz, HG_HEADS, HG_KEY, HG_VAL), _F32),
        ],
        scratch_shapes=[
            pltpu.VMEM((HG_HEADS // 2, 2 * HG_VAL, 2 * HG_KEY), _F32),
            pltpu.VMEM((HG_HEADS // 2, c, 2 * HG_KEY), _F32),
        ],
        compiler_params=pltpu.CompilerParams(
            dimension_semantics=("arbitrary", "arbitrary"), vmem_limit_bytes=V7X_VMEM_LIMIT_BYTES),
        name="hgrn_prompt",
    )(hg, lb_gamma, hg_g, jnp.asarray(mats, _BF16), jnp.asarray(lvl))


def _hgrn_sample_kernel(hg_ref, lbg_ref, g_ref, mask_ref, s0_ref, o_ref, s_ref):
    nb = SAMPLE_GROUP
    lb = _lower_bound(lbg_ref[...])
    q, cb, k, v, gate = [], [], [], [], []
    for t in range(DEC_SEQ):
        qt, lf, kt, vt, gt = _hgrn_gates(hg_ref[t], lb)
        q.append(qt); k.append(kt); v.append(vt); gate.append(gt)
        cb.append(lf if t == 0 else cb[-1] + lf)
    total = cb[-1]
    mask = mask_ref[...]
    for h in range(HG_HEADS):
        hs = slice(h * HG_KEY, (h + 1) * HG_KEY)
        qdec = jnp.concatenate([q[t][:, hs] * jnp.exp(cb[t][:, hs]) for t in range(DEC_SEQ)], axis=0)
        kend = jnp.concatenate(
            [k[t][:, hs] * jnp.exp(total[:, hs] - cb[t][:, hs]) for t in range(DEC_SEQ)], axis=0)
        vall = jnp.concatenate([v[t][:, hs] for t in range(DEC_SEQ)], axis=0).astype(_BF16)
        qblk = (jnp.concatenate([qdec] * nb, axis=1) * mask).astype(_BF16)
        kblk = (jnp.concatenate([kend] * nb, axis=1) * mask).astype(_BF16)
        s0 = s0_ref[:, h]
        o_inter = _dot(qblk, s0.reshape(nb * HG_KEY, HG_VAL).astype(_BF16))
        upd = _dot_tn(kblk, vall)
        decay_t = jnp.exp(total[:, hs]).T
        for b in range(nb):
            s_ref[b, h] = s0[b] * decay_t[:, b:b + 1] + upd[b * HG_KEY:(b + 1) * HG_KEY]
        for t in range(DEC_SEQ):
            o = o_inter[t * nb:(t + 1) * nb]
            for s in range(t + 1):
                w = q[t][:, hs] * k[s][:, hs]
                if s < t:
                    w = w * jnp.exp(cb[t][:, hs] - cb[s][:, hs])
                o = o + jnp.sum(w, axis=-1, keepdims=True) * v[s][:, hs]
            o = o * lax.rsqrt(jnp.mean(o * o, axis=-1, keepdims=True) + RMS_EPS)
            o_ref[t, :, hs] = o * g_ref[:, hs] * gate[t][:, hs]


def _hgrn_sample(hg, s0, lb_gamma, hg_g):
    l, db, _ = hg.shape
    nb = SAMPLE_GROUP
    rb = np.arange(l * nb)[:, None] % nb
    cbk = np.arange(nb * HG_KEY)[None, :] // HG_KEY
    mask = jnp.asarray((rb == cbk).astype(np.float32))
    return pl.pallas_call(
        _hgrn_sample_kernel,
        grid=(db // nb,),
        in_specs=[
            pl.BlockSpec((l, nb, HG_SLAB), lambda i: (0, i, 0)),
            pl.BlockSpec(lb_gamma.shape, lambda i: (0, 0)),
            pl.BlockSpec((1, HG_WIDTH), lambda i: (0, 0)),
            pl.BlockSpec(mask.shape, lambda i: (0, 0)),
            pl.BlockSpec((nb, HG_HEADS, HG_KEY, HG_VAL), lambda i: (i, 0, 0, 0)),
        ],
        out_specs=[
            pl.BlockSpec((l, nb, HG_WIDTH), lambda i: (0, i, 0)),
            pl.BlockSpec((nb, HG_HEADS, HG_KEY, HG_VAL), lambda i: (i, 0, 0, 0)),
        ],
        out_shape=[
            jax.ShapeDtypeStruct((l, db, HG_WIDTH), _F32),
            jax.ShapeDtypeStruct((db, HG_HEADS, HG_KEY, HG_VAL), _F32),
        ],
        compiler_params=pltpu.CompilerParams(
            dimension_semantics=("arbitrary",), vmem_limit_bytes=V7X_VMEM_LIMIT_BYTES),
        name="hgrn_sample",
    )(hg, lb_gamma, hg_g, mask, s0)


def _ffn_kernel(*refs, shift, blocks_per_seq, has_prev):
    if has_prev:
        (x_ref, oa_ref, oh_ref, ag_ref, wo_ref, n2_ref, wfi_ref, cw_ref, cb_ref, wfo_ref, fg_ref,
         prev_ref, y_ref, conv_ref, buf, hid) = refs
    else:
        (x_ref, oa_ref, oh_ref, ag_ref, wo_ref, n2_ref, wfi_ref, cw_ref, cb_ref, wfo_ref, fg_ref,
         y_ref, conv_ref, buf, hid) = refs
    tm = x_ref.shape[0]
    pad = buf.shape[0] - tm
    i = pl.program_id(0)

    oa = _rms(oa_ref[...], ag_ref[...])
    mix = jnp.concatenate([oa.astype(_BF16), oh_ref[...].astype(_BF16)], axis=-1)
    x1 = x_ref[...] + _dot(mix, wo_ref[...])
    h2 = _rms(x1, n2_ref[...]).astype(_BF16)

    if has_prev:
        buf[:pad, :] = prev_ref[...]
    else:
        @pl.when(i % blocks_per_seq == 0)
        def _():
            buf[:pad, :] = jnp.zeros((pad, D_FF), _F32)

        @pl.when(i % blocks_per_seq != 0)
        def _():
            buf[:pad, :] = buf[tm:tm + pad, :]

    for c in range(D_FF // FFN_COLS):
        cs = slice(c * FFN_COLS, (c + 1) * FFN_COLS)
        a = _dot(h2, wfi_ref[:, cs])
        gate = _dot(h2, wfi_ref[:, D_FF + c * FFN_COLS:D_FF + (c + 1) * FFN_COLS])
        buf[pad:, cs] = a
        ac = cb_ref[:, cs] + buf[pad - 2 * shift:pad - 2 * shift + tm, cs] * cw_ref[0:1, cs]
        ac = ac + buf[pad - shift:pad - shift + tm, cs] * cw_ref[1:2, cs]
        ac = ac + a * cw_ref[2:3, cs]
        hid[:, cs] = (jax.nn.silu(ac) * gate).astype(_BF16)
    y_ref[...] = _rms(x1 + _dot(hid[...], wfo_ref[...]), fg_ref[...])
    conv_ref[...] = buf[pad + tm - 2 * shift:, :]


def _ffn(x, oa, oh, attn_g, w_o, n2, wf_in, conv_w, conv_b, wf_out, final_g, *, shift,
         rows_per_seq, prev=None):
    t = x.shape[0]
    tm = FFN_ROWS
    nblk = t // tm
    blocks_per_seq = rows_per_seq // tm
    nseq = nblk // blocks_per_seq
    pad = max(V7X_SUBLANES, 2 * shift)
    tail = 2 * shift
    const = functools.partial(pl.BlockSpec, pipeline_mode=pl.Buffered(1))
    in_specs = [
        pl.BlockSpec((tm, D_MODEL), lambda i: (i, 0)),
        pl.BlockSpec((tm, ATT_WIDTH), lambda i: (i, 0)),
        pl.BlockSpec((tm, HG_WIDTH), lambda i: (i, 0)),
        const((1, ATT_WIDTH), lambda i: (0, 0)),
        const((ATT_WIDTH + HG_WIDTH, D_MODEL), lambda i: (0, 0)),
        const((1, D_MODEL), lambda i: (0, 0)),
        const((D_MODEL, 2 * D_FF), lambda i: (0, 0)),
        const((CONV_W, D_FF), lambda i: (0, 0)),
        const((1, D_FF), lambda i: (0, 0)),
        const((D_FF, D_MODEL), lambda i: (0, 0)),
        const((1, D_MODEL), lambda i: (0, 0)),
    ]
    args = [x, oa, oh, attn_g, w_o, n2, wf_in, conv_w, conv_b, wf_out, final_g]
    if prev is not None:
        in_specs.append(const((pad, D_FF), lambda i: (0, 0)))
        args.append(prev)
    return pl.pallas_call(
        functools.partial(_ffn_kernel, shift=shift, blocks_per_seq=blocks_per_seq,
                          has_prev=prev is not None),
        grid=(nblk,),
        in_specs=in_specs,
        out_specs=[
            pl.BlockSpec((tm, D_MODEL), lambda i: (i, 0)),
            pl.BlockSpec((None, tail, D_FF), lambda i: (i // blocks_per_seq, 0, 0)),
        ],
        out_shape=[
            jax.ShapeDtypeStruct((t, D_MODEL), _F32),
            jax.ShapeDtypeStruct((nseq, tail, D_FF), _F32),
        ],
        scratch_shapes=[pltpu.VMEM((pad + tm, D_FF), _F32), pltpu.VMEM((tm, D_FF), _BF16)],
        compiler_params=pltpu.CompilerParams(
            dimension_semantics=("arbitrary",), vmem_limit_bytes=V7X_VMEM_LIMIT_BYTES),
        name="ffn",
    )(*args)


def _interleave(streams):
    done = [0] * len(streams)
    live = set(range(len(streams)))
    while live:
        idx = min(live, key=lambda s: (done[s] + 1) / streams[s][1])
        try:
            next(streams[idx][0])
            done[idx] += 1
        except StopIteration:
            live.discard(idx)


def _attn_stream(q_ref, kvp_ref, kvc_ref, bias_scr, ag_ref, out_ref, seq_start):
    w = WINDOW
    pair = 2 * ATT_HEAD_DIM
    half_g = ATT_GROUP // 2
    rows = q_ref.shape[0]
    kvx = jnp.concatenate([kvp_ref[...], kvc_ref[...]], axis=0)
    kx = kvx[:, :pair]
    kr = pltpu.roll(kx, ATT_HEAD_DIM, 1).astype(_BF16)
    kx = kx.astype(_BF16)
    lo = lax.broadcasted_iota(jnp.int32, kx.shape, 1) < ATT_HEAD_DIM
    zero = jnp.zeros_like(kx)
    kmat = ((jnp.where(lo, kx, zero), jnp.where(lo, kr, zero)),
            (jnp.where(lo, zero, kr), jnp.where(lo, zero, kx)))
    v_t = kvx[:, pair:].T.astype(_BF16)
    row = lax.broadcasted_iota(jnp.int32, (2 * w, half_g * w), 0)
    first_key = lax.broadcasted_iota(jnp.int32, (2 * w, pair), 0) == 0
    first_col = lax.broadcasted_iota(jnp.int32, (ATT_HEAD_DIM, 2 * w), 1) == 0
    no_prev = (row >= 1) & (row < w) & seq_start
    units = [(kv, par) for kv in range(ATT_KV_HEADS) for par in range(2)]
    yield
    for j in range(rows // w):
        qs = slice(j * w, (j + 1) * w)
        ks = slice(j * w, (j + 2) * w)
        scores = {}
        for kv in range(ATT_KV_HEADS):
            base = kv * ATT_GROUP * ATT_HEAD_DIM
            lhs = jnp.concatenate(
                [q_ref[qs, base + i * pair:base + (i + 1) * pair] for i in range(half_g)], axis=0)
            lhs = (lhs * (ATT_HEAD_DIM ** -0.5)).astype(_BF16)
            for par in range(2):
                keys = jnp.where(first_key, jnp.zeros((), _BF16), kmat[par][kv][ks])
                scores[kv, par] = _dot_nt(keys, lhs)
        yield
        probs = {}
        for kv, par in units:
            s = scores[kv, par] + bias_scr[kv, par]
            if j == 0:
                s = jnp.where(no_prev, _NEG_INF, s)
            m = jnp.max(s, axis=0, keepdims=True)
            p = jnp.exp(s - m)
            probs[kv, par] = (p.astype(_BF16), jnp.sum(p, axis=0, keepdims=True))
        yield
        outs = {}
        for kv, par in units:
            p, den = probs[kv, par]
            vals = v_t[kv * ATT_HEAD_DIM:(kv + 1) * ATT_HEAD_DIM, ks]
            vals = jnp.where(first_col, jnp.zeros((), _BF16), vals)
            outs[kv, par] = _dot(vals, p) / den
        pieces = []
        for kv in range(ATT_KV_HEADS):
            for i in range(half_g):
                both = jnp.concatenate(
                    [outs[kv, par][:, i * w:(i + 1) * w] for par in range(2)], axis=0)
                pieces.append(both.T)
        o = jnp.concatenate(pieces, axis=-1)
        out_ref[qs, :ATT_WIDTH] = _rms(o, ag_ref[...]).astype(_BF16)
        yield


def _hgrn_stream(hg_ref, lbg_ref, g_ref, m_ref, lvl_ref, st_scr, cb_scr, out_ref, seq_start):
    c = HG_CHUNK
    pw = 2 * HG_KEY
    pairs = range(HG_HEADS // 2)
    lb = _lower_bound(lbg_ref[...])
    lvl = lvl_ref[...]
    rr = lax.broadcasted_iota(jnp.int32, (pw, pw), 0) < HG_VAL
    cc = lax.broadcasted_iota(jnp.int32, (pw, pw), 1) < HG_KEY
    same_head = rr == cc
    for ci in range(hg_ref.shape[0] // c):
        rs = slice(ci * c, (ci + 1) * c)
        q, k, v, gate, sums, args = {}, {}, {}, {}, {}, {}
        for p in pairs:
            ps = slice(p * pw, (p + 1) * pw)
            q[p] = jax.nn.silu(hg_ref[rs, p * pw:(p + 1) * pw])
            f = lb[:, ps] + (1.0 - lb[:, ps]) * jax.nn.sigmoid(
                hg_ref[rs, HG_WIDTH + p * pw:HG_WIDTH + (p + 1) * pw])
            v[p] = hg_ref[rs, 2 * HG_WIDTH + p * pw:2 * HG_WIDTH + (p + 1) * pw]
            gate[p] = jax.nn.silu(hg_ref[rs, 3 * HG_WIDTH + p * pw:3 * HG_WIDTH + (p + 1) * pw])
            k[p] = 1.0 - f
            sums[p] = _exact_rows_matmul(m_ref[...], jnp.log(f) * _LOG2_E)
        yield
        for p in pairs:
            cb_scr[p] = sums[p][:c]
            args[p] = _hgrn_level_exponents(sums[p], cb_scr.at[p], c)
        yield
        a = {p: jnp.zeros((c, pw), _F32) for p in pairs}
        for li in range(len(_hgrn_levels(c))):
            for p in pairs:
                e = jnp.exp2(args[p][li])
                pr = _dot_nt((q[p] * e).astype(_BF16), _pair_blockdiag((k[p] * e).astype(_BF16)))
                a[p] = jnp.where(lvl == li, pr, a[p])
            yield
        for p in pairs:
            cb = sums[p][:c]
            total = cb[c - 1:c, :]
            vb = v[p].astype(_BF16)
            st = st_scr[p]
            if ci == 0:
                st = jnp.where(seq_start, 0.0, st)
            o = (_dot_nt((q[p] * jnp.exp2(cb)).astype(_BF16), st.astype(_BF16))
                 + _dot(a[p].astype(_BF16), _pair_blockdiag(vb)))
            upd = _dot_tn(vb, (k[p] * jnp.exp2(total - cb)).astype(_BF16))
            st_scr[p] = st * jnp.exp2(total) + jnp.where(same_head, upd, 0.0)
            qk = q[p] * k[p]
            for hh in range(2):
                hs = slice(hh * HG_KEY, (hh + 1) * HG_KEY)
                gs = slice(p * pw + hh * HG_KEY, p * pw + (hh + 1) * HG_KEY)
                oh = o[:, hs] + jnp.sum(qk[:, hs], axis=-1, keepdims=True) * v[p][:, hs]
                oh = oh * lax.rsqrt(jnp.mean(oh * oh, axis=-1, keepdims=True) + RMS_EPS)
                out_ref[rs, ATT_WIDTH + gs.start:ATT_WIDTH + gs.stop] = (
                    oh * g_ref[:, gs] * gate[p][:, hs]).astype(_BF16)
        yield


def _ffn_stream(mix_ref, x_ref, wo_ref, n2_ref, wfi_ref, cw_ref, cb_ref, wfo_ref, fg_ref,
                y_ref, conv_ref, buf, hid, seq_start):
    tm = x_ref.shape[0]
    pad = buf.shape[0] - tm
    x1 = x_ref[...] + _dot(mix_ref[...], wo_ref[...])
    h2 = _rms(x1, n2_ref[...]).astype(_BF16)
    buf[:pad, :] = jnp.where(seq_start, 0.0, buf[tm:tm + pad, :])
    yield
    for c in range(D_FF // FFN_COLS):
        cs = slice(c * FFN_COLS, (c + 1) * FFN_COLS)
        a = _dot(h2, wfi_ref[:, cs])
        gate = _dot(h2, wfi_ref[:, D_FF + c * FFN_COLS:D_FF + (c + 1) * FFN_COLS])
        buf[pad:, cs] = a
        ac = cb_ref[:, cs] + buf[pad - 2:pad - 2 + tm, cs] * cw_ref[0:1, cs]
        ac = ac + buf[pad - 1:pad - 1 + tm, cs] * cw_ref[1:2, cs]
        ac = ac + a * cw_ref[2:3, cs]
        hid[:, cs] = (jax.nn.silu(ac) * gate).astype(_BF16)
        yield
    parts = []
    for n in range(D_MODEL // FFN_COLS):
        ns = slice(n * FFN_COLS, (n + 1) * FFN_COLS)
        parts.append(x1[:, ns] + _dot(hid[...], wfo_ref[:, ns]))
        yield
    y_ref[...] = _rms(jnp.concatenate(parts, axis=-1), fg_ref[...])
    conv_ref[...] = buf[pad + tm - (CONV_W - 1):, :]


def _layer_kernel(relb_ref, sink_ref, bidx_ref, q_ref, kvp_ref, kvc_ref, hg_ref, lbg_ref, hgg_ref,
                  m_ref, lvl_ref, x_ref, ag_ref, wo_ref, n2_ref, wfi_ref, cw_ref, cb_ref, wfo_ref,
                  fg_ref, y_ref, conv_ref, s_ref, bias_scr, st_scr, cb_scr, mix_scr, buf, hid, *,
                  blocks_per_seq):
    i = pl.program_id(0)
    nblk = pl.num_programs(0) - 1
    w = WINDOW

    @pl.when(i == 0)
    def _():
        bidx = bidx_ref[...]
        key0 = lax.broadcasted_iota(jnp.int32, bidx.shape, 0) == 0
        for kv in range(ATT_KV_HEADS):
            for par in range(2):
                for g in range(ATT_GROUP // 2):
                    h = kv * ATT_GROUP + 2 * g + par
                    tab = _bias_from_buckets(bidx, relb_ref, h)
                    bias_scr[kv, par, :, g * w:(g + 1) * w] = jnp.where(key0, sink_ref[h], tab)
        st_scr[...] = jnp.zeros(st_scr.shape, _F32)
        mix_scr[...] = jnp.zeros(mix_scr.shape, _BF16)
        buf[...] = jnp.zeros(buf.shape, _F32)

    slot = i % 2
    mix_start = (jnp.minimum(i, nblk - 1) % blocks_per_seq) == 0
    ffn_start = (jnp.maximum(i - 1, 0) % blocks_per_seq) == 0
    rows = q_ref.shape[0]
    chunks = rows // HG_CHUNK
    _interleave([
        (_ffn_stream(mix_scr.at[1 - slot], x_ref, wo_ref, n2_ref, wfi_ref, cw_ref, cb_ref, wfo_ref,
                     fg_ref, y_ref, conv_ref, buf, hid, ffn_start),
         1 + D_FF // FFN_COLS + D_MODEL // FFN_COLS + 1),
        (_hgrn_stream(hg_ref, lbg_ref, hgg_ref, m_ref, lvl_ref, st_scr, cb_scr, mix_scr.at[slot],
                      mix_start), chunks * (3 + len(_hgrn_levels(HG_CHUNK)))),
        (_attn_stream(q_ref, kvp_ref, kvc_ref, bias_scr, ag_ref, mix_scr.at[slot], mix_start),
         1 + 3 * (rows // w)),
    ])

    @pl.when((i % blocks_per_seq == blocks_per_seq - 1) & (i < nblk))
    def _():
        for p in range(HG_HEADS // 2):
            st = st_scr[p]
            s_ref[2 * p] = st[:HG_VAL, :HG_KEY].T
            s_ref[2 * p + 1] = st[HG_VAL:, HG_KEY:].T


def _layer(x, att, hg, rel_bias, sinks, lb_gamma, hg_g, attn_g, w_o, n2, wf_in, conv_w, conv_b,
           wf_out, final_g, *, rows_per_seq):
    t = x.shape[0]
    rows = LAYER_ROWS
    nblk = t // rows
    bps = rows_per_seq // rows
    nseq = nblk // bps
    pad = V7X_SUBLANES
    qi = np.arange(WINDOW)[None, :]
    si = np.arange(2 * WINDOW)[:, None]
    bidx = jnp.asarray(_t5_bucket_np(WINDOW + qi - si))
    mats, lvl = _hgrn_tables(HG_CHUNK)
    kvcol = ATT_WIDTH // (2 * KV_WIDTH)
    wblocks = rows // WINDOW
    last = nblk - 1
    smem = pl.BlockSpec(memory_space=pltpu.SMEM)
    const = functools.partial(pl.BlockSpec, pipeline_mode=pl.Buffered(1))

    def mix_blk(i):
        return jnp.minimum(i, last)

    def ffn_blk(i):
        return jnp.maximum(i - 1, 0)

    in_specs = [
        smem, smem,
        const(bidx.shape, lambda i: (0, 0)),
        pl.BlockSpec((rows, ATT_WIDTH), lambda i: (mix_blk(i), 0)),
        pl.BlockSpec((WINDOW, 2 * KV_WIDTH),
                     lambda i: (jnp.maximum(mix_blk(i) * wblocks - 1, 0), kvcol)),
        pl.BlockSpec((rows, 2 * KV_WIDTH), lambda i: (mix_blk(i), kvcol)),
        pl.BlockSpec((rows, HG_SLAB), lambda i: (mix_blk(i), 0)),
        const(lb_gamma.shape, lambda i: (0, 0)),
        const((1, HG_WIDTH), lambda i: (0, 0)),
        const(mats.shape, lambda i: (0, 0)),
        const(lvl.shape, lambda i: (0, 0)),
        pl.BlockSpec((rows, D_MODEL), lambda i: (ffn_blk(i), 0)),
        const((1, ATT_WIDTH), lambda i: (0, 0)),
        const((ATT_WIDTH + HG_WIDTH, D_MODEL), lambda i: (0, 0)),
        const((1, D_MODEL), lambda i: (0, 0)),
        const((D_MODEL, 2 * D_FF), lambda i: (0, 0)),
        const((CONV_W, D_FF), lambda i: (0, 0)),
        const((1, D_FF), lambda i: (0, 0)),
        const((D_FF, D_MODEL), lambda i: (0, 0)),
        const((1, D_MODEL), lambda i: (0, 0)),
    ]
    return pl.pallas_call(
        functools.partial(_layer_kernel, blocks_per_seq=bps),
        grid=(nblk + 1,),
        in_specs=in_specs,
        out_specs=[
            pl.BlockSpec((rows, D_MODEL), lambda i: (ffn_blk(i), 0)),
            pl.BlockSpec((None, CONV_W - 1, D_FF), lambda i: (ffn_blk(i) // bps, 0, 0)),
            pl.BlockSpec((None, HG_HEADS, HG_KEY, HG_VAL), lambda i: (mix_blk(i) // bps, 0, 0, 0)),
        ],
        out_shape=[
            jax.ShapeDtypeStruct((t, D_MODEL), _F32),
            jax.ShapeDtypeStruct((nseq, CONV_W - 1, D_FF), _F32),
            jax.ShapeDtypeStruct((nseq, HG_HEADS, HG_KEY, HG_VAL), _F32),
        ],
        scratch_shapes=[
            pltpu.VMEM((ATT_KV_HEADS, 2, 2 * WINDOW, ATT_GROUP // 2 * WINDOW), _F32),
            pltpu.VMEM((HG_HEADS // 2, 2 * HG_VAL, 2 * HG_KEY), _F32),
            pltpu.VMEM((HG_HEADS // 2, HG_CHUNK, 2 * HG_KEY), _F32),
            pltpu.VMEM((2, rows, ATT_WIDTH + HG_WIDTH), _BF16),
            pltpu.VMEM((pad + rows, D_FF), _F32),
            pltpu.VMEM((rows, D_FF), _BF16),
        ],
        compiler_params=pltpu.CompilerParams(
            dimension_semantics=("arbitrary",), vmem_limit_bytes=V7X_VMEM_LIMIT_BYTES),
        name="layer",
    )(rel_bias.reshape(-1), sinks, bidx, att, att, att, hg, lb_gamma, hg_g, jnp.asarray(mats, _BF16),
      jnp.asarray(lvl), x, attn_g, w_o, n2, wf_in, conv_w, conv_b, wf_out, final_g)


def kernel(x_prompt, x_sample, cache_k_win, cache_v_win, state_hgrn, state_conv, norm1_g, w_in,
           attn_sinks, rel_bias, lb_gamma, attn_out_g, hg_out_g, w_out, norm2_g, w_ffn_in, conv_w,
           conv_b, w_ffn_out, final_g):
    assert DEPTH == 1
    n1 = norm1_g[0][None]
    n2 = norm2_g[0][None]
    ag = attn_out_g[0][None]
    hgg = hg_out_g[0][None]
    fg = final_g[None]
    cb = conv_b[0][None]
    cw = conv_w[0]
    sinks = attn_sinks[0]
    w_in_b = w_in[0].astype(_BF16)
    w_o_b = w_out[0].astype(_BF16)
    wf_in_b = w_ffn_in[0].astype(_BF16)
    wf_out_b = w_ffn_out[0].astype(_BF16)
    ffn_w = (ag, w_o_b, n2, wf_in_b, cw, cb, wf_out_b, fg)

    xp = x_prompt.reshape(BATCH * SEQ, D_MODEL)
    att_p, hg_p = _proj(xp, n1, w_in_b)
    y_p, conv_p, s_p = _layer(xp, att_p, hg_p, rel_bias, sinks, lb_gamma, hgg, *ffn_w,
                              rows_per_seq=SEQ)
    kv_tail = att_p.reshape(BATCH, SEQ, ATT_SLAB)[:, SEQ - WINDOW:, ATT_WIDTH:]
    new_k_p = kv_tail[:, :, :KV_WIDTH].reshape(1, BATCH, WINDOW, ATT_KV_HEADS, ATT_HEAD_DIM)
    new_v_p = kv_tail[:, :, KV_WIDTH:].reshape(1, BATCH, WINDOW, ATT_KV_HEADS, ATT_HEAD_DIM)

    rows_s = DEC_SEQ * DEC_BATCH
    xs = x_sample.transpose(1, 0, 2).reshape(rows_s, D_MODEL)
    att_s, hg_s = _proj(xs, n1, w_in_b)
    att_s = att_s.reshape(DEC_SEQ, DEC_BATCH, ATT_SLAB)
    q_r = att_s[:, :, :ATT_WIDTH].reshape(DEC_SEQ, DEC_BATCH, ATT_KV_HEADS, ATT_GROUP, ATT_HEAD_DIM)
    q_r = q_r.transpose(1, 2, 3, 0, 4).reshape(DEC_BATCH, ATT_KV_HEADS, ATT_GROUP * DEC_SEQ, ATT_HEAD_DIM)
    q_zero = jnp.zeros_like(q_r[:, 0])
    q_r = jnp.stack([jnp.concatenate([q_r[:, 0], q_zero], axis=-1),
                     jnp.concatenate([q_zero, q_r[:, 1]], axis=-1)], axis=1)
    kv_new = att_s[:, :, ATT_WIDTH:].transpose(1, 0, 2)
    kv_new = jnp.pad(kv_new, ((0, 0), (0, SAMPLE_KEYS_PAD - SAMPLE_KEYS), (0, 0)))
    o16, new_k_s, new_v_s = _attn_sample(
        q_r, kv_new, cache_k_win[0].reshape(DEC_BATCH, WINDOW, KV_WIDTH),
        cache_v_win[0].reshape(DEC_BATCH, WINDOW, KV_WIDTH), rel_bias, sinks)
    oa_s = o16.reshape(DEC_BATCH, ATT_KV_HEADS, ATT_GROUP, DEC_SEQ, ATT_HEAD_DIM)
    oa_s = oa_s.transpose(3, 0, 1, 2, 4).reshape(rows_s, ATT_WIDTH)
    oh_s, s_s = _hgrn_sample(hg_s.reshape(DEC_SEQ, DEC_BATCH, HG_SLAB), state_hgrn[0], lb_gamma, hgg)
    prev = state_conv[0].transpose(1, 0, 2).reshape((CONV_W - 1) * DEC_BATCH, D_FF)
    y_s, conv_s = _ffn(xs, oa_s, oh_s.reshape(rows_s, HG_WIDTH), *ffn_w, shift=DEC_BATCH,
                       rows_per_seq=rows_s, prev=prev)
    y_s = y_s.reshape(DEC_SEQ, DEC_BATCH, D_MODEL).transpose(1, 0, 2)
    conv_s = conv_s.reshape(CONV_W - 1, DEC_BATCH, D_FF).transpose(1, 0, 2)

    cache_shape = (1, DEC_BATCH, WINDOW, ATT_KV_HEADS, ATT_HEAD_DIM)
    return (y_p.reshape(BATCH, SEQ, D_MODEL), y_s, new_k_p, new_v_p, s_p[None], conv_p[None],
            new_k_s.reshape(cache_shape), new_v_s.reshape(cache_shape), s_s[None], conv_s[None])
```

```python
import functools
import math

import jax
import jax.numpy as jnp
import numpy as np
from jax import lax
from jax.experimental import pallas as pl
from jax.experimental.pallas import tpu as pltpu

D_MODEL = 1024
BATCH = 2
SEQ = 8192
DEPTH = 1
DEC_BATCH = 128
DEC_SEQ = 4
ATT_HEAD_DIM = 64
ATT_HEADS = 8
ATT_KV_HEADS = 2
ATT_GROUP = 4
ATT_WIDTH = 512
KV_WIDTH = ATT_KV_HEADS * ATT_HEAD_DIM
WINDOW = 128
NUM_BUCKETS = 32
MAX_DISTANCE = 128
HG_KEY = 128
HG_VAL = 128
HG_HEADS = 4
HG_WIDTH = 512
D_FF = 2816
CONV_W = 3
RMS_EPS = 1e-6
ATT_SLAB = ATT_WIDTH + 2 * KV_WIDTH
HG_SLAB = 4 * HG_WIDTH
PROJ_WIDTH = ATT_SLAB + HG_SLAB

V7X_SUBLANES = 8
V7X_LANES = 128
V7X_MXU_DIM = 256
V7X_VMEM_LIMIT_BYTES = 60 * 1024 * 1024

PROJ_ROWS = 512
LAYER_ROWS = 512
ATT_ROWS = 512
FFN_ROWS = 512
FFN_COLS = V7X_MXU_DIM
HG_CHUNK = 128
SAMPLE_GROUP = 8
SAMPLE_KEYS = WINDOW + DEC_SEQ
SAMPLE_KEYS_PAD = SAMPLE_KEYS + (-SAMPLE_KEYS) % V7X_SUBLANES

_F32 = jnp.float32
_BF16 = jnp.bfloat16
_NEG_INF = float("-inf")
_LOG2_E = 1.0 / math.log(2.0)


def _dot(a, b):
    return jnp.dot(a, b, preferred_element_type=_F32)


def _dot_nt(a, b):
    return lax.dot_general(a, b, (((1,), (1,)), ((), ())), preferred_element_type=_F32)


def _dot_tn(a, b):
    return lax.dot_general(a, b, (((0,), (0,)), ((), ())), preferred_element_type=_F32)


def _rms(x, g):
    return x * lax.rsqrt(jnp.mean(x * x, axis=-1, keepdims=True) + RMS_EPS) * g


def _t5_bucket_np(dist):
    dist = np.asarray(dist)
    n = np.maximum(dist, 0)
    max_exact = NUM_BUCKETS // 2
    nf = np.maximum(n, 1).astype(np.float64)
    large = max_exact + (np.log(nf / max_exact) / math.log(MAX_DISTANCE / max_exact)
                         * (NUM_BUCKETS - max_exact)).astype(np.int32)
    large = np.minimum(large, NUM_BUCKETS - 1)
    bucket = np.where(n < max_exact, n, large)
    return np.where((dist >= 0) & (dist < WINDOW), bucket, -1).astype(np.int32)


def _bias_from_buckets(bidx, relb_ref, head):
    def body(b, acc):
        return jnp.where(bidx == b, relb_ref[b * ATT_HEADS + head], acc)
    return lax.fori_loop(0, NUM_BUCKETS, body, jnp.full(bidx.shape, _NEG_INF, _F32))


def _proj_kernel(x_ref, g_ref, w_ref, att_ref, hg_ref):
    h = _rms(x_ref[...], g_ref[...]).astype(_BF16)
    res = _dot(h, w_ref[...])
    att_ref[...] = res[:, :ATT_SLAB]
    hg_ref[...] = res[:, ATT_SLAB:]


def _proj(x, g, w):
    t = x.shape[0]
    tm = PROJ_ROWS
    return pl.pallas_call(
        _proj_kernel,
        grid=(t // tm,),
        in_specs=[
            pl.BlockSpec((tm, D_MODEL), lambda i: (i, 0)),
            pl.BlockSpec((1, D_MODEL), lambda i: (0, 0)),
            pl.BlockSpec((D_MODEL, PROJ_WIDTH), lambda i: (0, 0)),
        ],
        out_specs=[
            pl.BlockSpec((tm, ATT_SLAB), lambda i: (i, 0)),
            pl.BlockSpec((tm, HG_SLAB), lambda i: (i, 0)),
        ],
        out_shape=[
            jax.ShapeDtypeStruct((t, ATT_SLAB), _F32),
            jax.ShapeDtypeStruct((t, HG_SLAB), _F32),
        ],
        compiler_params=pltpu.CompilerParams(
            dimension_semantics=("arbitrary",), vmem_limit_bytes=V7X_VMEM_LIMIT_BYTES),
        name="proj",
    )(x, g, w)


def _attn_prompt_kernel(relb_ref, sink_ref, bidx_ref, q_ref, kvp_ref, kvc_ref, o_ref, bias_scr):
    b = pl.program_id(0)
    n = pl.program_id(1)
    w = WINDOW
    pair = 2 * ATT_HEAD_DIM
    half_g = ATT_GROUP // 2

    @pl.when((b == 0) & (n == 0))
    def _():
        bidx = bidx_ref[...]
        key0 = lax.broadcasted_iota(jnp.int32, bidx.shape, 0) == 0
        for kv in range(ATT_KV_HEADS):
            for par in range(2):
                for i in range(half_g):
                    h = kv * ATT_GROUP + 2 * i + par
                    tab = _bias_from_buckets(bidx, relb_ref, h)
                    bias_scr[kv, par, :, i * w:(i + 1) * w] = jnp.where(key0, sink_ref[h], tab)

    kvx = jnp.concatenate([kvp_ref[...], kvc_ref[...]], axis=0)
    kx = kvx[:, :pair]
    kr = pltpu.roll(kx, ATT_HEAD_DIM, 1).astype(_BF16)
    kx = kx.astype(_BF16)
    lo = lax.broadcasted_iota(jnp.int32, kx.shape, 1) < ATT_HEAD_DIM
    zero = jnp.zeros_like(kx)
    kmat = ((jnp.where(lo, kx, zero), jnp.where(lo, kr, zero)),
            (jnp.where(lo, zero, kr), jnp.where(lo, zero, kx)))
    v_t = kvx[:, pair:].T.astype(_BF16)
    row = lax.broadcasted_iota(jnp.int32, (2 * w, half_g * w), 0)
    first_key = lax.broadcasted_iota(jnp.int32, (2 * w, pair), 0) == 0
    first_col = lax.broadcasted_iota(jnp.int32, (ATT_HEAD_DIM, 2 * w), 1) == 0
    no_prev = (row >= 1) & (row < w) & (n == 0)
    for j in range(ATT_ROWS // w):
        qs = slice(j * w, (j + 1) * w)
        ks = slice(j * w, (j + 2) * w)
        units = [(kv, par) for kv in range(ATT_KV_HEADS) for par in range(2)]
        scores = {}
        for kv in range(ATT_KV_HEADS):
            base = kv * ATT_GROUP * ATT_HEAD_DIM
            lhs = jnp.concatenate(
                [q_ref[qs, base + i * pair:base + (i + 1) * pair] for i in range(half_g)], axis=0)
            lhs = (lhs * (ATT_HEAD_DIM ** -0.5)).astype(_BF16)
            for par in range(2):
                keys = jnp.where(first_key, jnp.zeros((), _BF16), kmat[par][kv][ks])
                scores[kv, par] = _dot_nt(keys, lhs)
        probs = {}
        for kv, par in units:
            s = scores[kv, par] + bias_scr[kv, par]
            if j == 0:
                s = jnp.where(no_prev, _NEG_INF, s)
            m = jnp.max(s, axis=0, keepdims=True)
            p = jnp.exp(s - m)
            probs[kv, par] = (p.astype(_BF16), jnp.sum(p, axis=0, keepdims=True))
        outs = {}
        for kv, par in units:
            p, den = probs[kv, par]
            vals = v_t[kv * ATT_HEAD_DIM:(kv + 1) * ATT_HEAD_DIM, ks]
            vals = jnp.where(first_col, jnp.zeros((), _BF16), vals)
            outs[kv, par] = _dot(vals, p) / den
        for kv in range(ATT_KV_HEADS):
            base = kv * ATT_GROUP * ATT_HEAD_DIM
            for i in range(half_g):
                both = jnp.concatenate(
                    [outs[kv, par][:, i * w:(i + 1) * w] for par in range(2)], axis=0)
                o_ref[qs, base + i * pair:base + (i + 1) * pair] = both.T


def _attn_prompt(att, rel_bias, sinks):
    bsz, l, _ = att.shape
    rows = ATT_ROWS
    qi = np.arange(WINDOW)[None, :]
    si = np.arange(2 * WINDOW)[:, None]
    bidx = jnp.asarray(_t5_bucket_np(WINDOW + qi - si))
    kvcol = ATT_WIDTH // (2 * KV_WIDTH)
    blocks = rows // WINDOW
    smem = pl.BlockSpec(memory_space=pltpu.SMEM)
    return pl.pallas_call(
        _attn_prompt_kernel,
        grid=(bsz, l // rows),
        in_specs=[
            smem, smem,
            pl.BlockSpec((2 * WINDOW, WINDOW), lambda b, n: (0, 0)),
            pl.BlockSpec((None, rows, ATT_WIDTH), lambda b, n: (b, n, 0)),
            pl.BlockSpec((None, WINDOW, 2 * KV_WIDTH),
                         lambda b, n: (b, jnp.maximum(n * blocks - 1, 0), kvcol)),
            pl.BlockSpec((None, rows, 2 * KV_WIDTH), lambda b, n: (b, n, kvcol)),
        ],
        out_specs=pl.BlockSpec((None, rows, ATT_WIDTH), lambda b, n: (b, n, 0)),
        out_shape=jax.ShapeDtypeStruct((bsz, l, ATT_WIDTH), _F32),
        scratch_shapes=[pltpu.VMEM((ATT_KV_HEADS, 2, 2 * WINDOW, ATT_GROUP // 2 * WINDOW), _F32)],
        compiler_params=pltpu.CompilerParams(
            dimension_semantics=("arbitrary", "arbitrary"), vmem_limit_bytes=V7X_VMEM_LIMIT_BYTES),
        name="attn_prompt",
    )(rel_bias.reshape(-1), sinks, bidx, att, att, att)


def _attn_sample_kernel(relb_ref, sink_ref, bidx_ref, q_ref, kvn_ref, kc_ref, vc_ref,
                        o_ref, nk_ref, nv_ref, bias_scr, sink_scr):
    @pl.when(pl.program_id(0) == 0)
    def _():
        bidx = bidx_ref[...]
        row_g = lax.broadcasted_iota(jnp.int32, (ATT_GROUP * DEC_SEQ, V7X_LANES), 0) // DEC_SEQ
        for kv in range(ATT_KV_HEADS):
            bias = jnp.full(bidx.shape, _NEG_INF, _F32)
            sk = jnp.zeros(row_g.shape, _F32)
            for g in range(ATT_GROUP):
                h = kv * ATT_GROUP + g
                rows = lax.broadcasted_iota(jnp.int32, bidx.shape, 0) // DEC_SEQ == g
                bias = jnp.where(rows, _bias_from_buckets(bidx, relb_ref, h), bias)
                sk = jnp.where(row_g == g, sink_ref[h], sk)
            bias_scr[kv] = bias
            sink_scr[kv] = sk

    keep = WINDOW - DEC_SEQ
    units = [(b, kv) for b in range(SAMPLE_GROUP) for kv in range(ATT_KV_HEADS)]
    scores = {}
    for b in range(SAMPLE_GROUP):
        kvn = kvn_ref[b]
        nk_ref[b, :keep, :] = kc_ref[b, DEC_SEQ:, :]
        nv_ref[b, :keep, :] = vc_ref[b, DEC_SEQ:, :]
        nk_ref[b, keep:, :] = kvn[:DEC_SEQ, :KV_WIDTH]
        nv_ref[b, keep:, :] = kvn[:DEC_SEQ, KV_WIDTH:]
        k = jnp.concatenate([kc_ref[b], kvn[:, :KV_WIDTH]], axis=0).astype(_BF16)
        for kv in range(ATT_KV_HEADS):
            q = (q_ref[b, kv] * (ATT_HEAD_DIM ** -0.5)).astype(_BF16)
            scores[b, kv] = _dot_nt(q, k)
    probs = {}
    for b, kv in units:
        s = scores[b, kv] + bias_scr[kv]
        sink = sink_scr[kv][:, :1]
        m = jnp.maximum(jnp.max(s, axis=-1, keepdims=True), sink)
        p = jnp.exp(s - m)
        den = jnp.sum(p, axis=-1, keepdims=True) + jnp.exp(sink - m)
        probs[b, kv] = (p.astype(_BF16), den)
    for b in range(SAMPLE_GROUP):
        v = jnp.concatenate([vc_ref[b], kvn_ref[b][:, KV_WIDTH:]], axis=0).astype(_BF16)
        for kv in range(ATT_KV_HEADS):
            p, den = probs[b, kv]
            o = _dot(p, v) / den
            o_ref[b, kv] = o[:, kv * ATT_HEAD_DIM:(kv + 1) * ATT_HEAD_DIM]


def _attn_sample(q_r, kv_new, cache_k, cache_v, rel_bias, sinks):
    db = q_r.shape[0]
    rows = ATT_GROUP * DEC_SEQ
    new_rows = SAMPLE_KEYS_PAD - WINDOW
    t = (np.arange(rows) % DEC_SEQ)[:, None]
    c = np.arange(SAMPLE_KEYS_PAD)[None, :]
    dist = np.where(c < SAMPLE_KEYS, t + WINDOW - c, -1)
    bidx = jnp.asarray(_t5_bucket_np(dist))
    smem = pl.BlockSpec(memory_space=pltpu.SMEM)
    g = SAMPLE_GROUP
    return pl.pallas_call(
        _attn_sample_kernel,
        grid=(db // g,),
        in_specs=[
            smem, smem,
            pl.BlockSpec((rows, SAMPLE_KEYS_PAD), lambda i: (0, 0)),
            pl.BlockSpec((g, ATT_KV_HEADS, rows, KV_WIDTH), lambda i: (i, 0, 0, 0)),
            pl.BlockSpec((g, new_rows, 2 * KV_WIDTH), lambda i: (i, 0, 0)),
            pl.BlockSpec((g, WINDOW, KV_WIDTH), lambda i: (i, 0, 0)),
            pl.BlockSpec((g, WINDOW, KV_WIDTH), lambda i: (i, 0, 0)),
        ],
        out_specs=[
            pl.BlockSpec((g, ATT_KV_HEADS, rows, ATT_HEAD_DIM), lambda i: (i, 0, 0, 0)),
            pl.BlockSpec((g, WINDOW, KV_WIDTH), lambda i: (i, 0, 0)),
            pl.BlockSpec((g, WINDOW, KV_WIDTH), lambda i: (i, 0, 0)),
        ],
        out_shape=[
            jax.ShapeDtypeStruct((db, ATT_KV_HEADS, rows, ATT_HEAD_DIM), _F32),
            jax.ShapeDtypeStruct((db, WINDOW, KV_WIDTH), _F32),
            jax.ShapeDtypeStruct((db, WINDOW, KV_WIDTH), _F32),
        ],
        scratch_shapes=[
            pltpu.VMEM((ATT_KV_HEADS, rows, SAMPLE_KEYS_PAD), _F32),
            pltpu.VMEM((ATT_KV_HEADS, rows, V7X_LANES), _F32),
        ],
        compiler_params=pltpu.CompilerParams(dimension_semantics=("arbitrary",)),
        name="attn_sample",
    )(rel_bias.reshape(-1), sinks, bidx, q_r, kv_new, cache_k, cache_v)


def _lower_bound(lbg):
    mx = jnp.max(lbg, axis=0, keepdims=True)
    e = jnp.exp(lbg - mx)
    return e[0:1] / jnp.sum(e, axis=0, keepdims=True)


def _hgrn_gates(hg, lb):
    q = jax.nn.silu(hg[:, 0:HG_WIDTH])
    f = lb + (1.0 - lb) * jax.nn.sigmoid(hg[:, HG_WIDTH:2 * HG_WIDTH])
    v = hg[:, 2 * HG_WIDTH:3 * HG_WIDTH]
    gate = jax.nn.silu(hg[:, 3 * HG_WIDTH:4 * HG_WIDTH])
    return q, jnp.log(f), 1.0 - f, v, gate


def _hgrn_levels(c):
    return [c >> (i + 1) for i in range(int(math.log2(c)))]


def _hgrn_small_levels(c):
    return [h for h in _hgrn_levels(c) if 2 * h < V7X_SUBLANES]


def _hgrn_tables(c):
    lvl = np.full((c, c), -1, np.int32)
    j = np.arange(c)[None, :]
    t = np.arange(c)[:, None]
    mats = [j <= t]
    for li, h in enumerate(_hgrn_levels(c)):
        mid = (t // (2 * h)) * (2 * h) + h
        same = (t // (2 * h)) == (j // (2 * h))
        lvl[same & (t >= mid) & (j < mid)] = li
        if h in _hgrn_small_levels(c):
            mats.append(np.where(t >= mid, (j >= mid) & (j <= t), (j > t) & (j < mid)))
    return np.concatenate(mats, axis=0).astype(np.float32), np.concatenate([lvl, lvl], axis=1)


def _exact_rows_matmul(m, x):
    h1 = x.astype(_BF16)
    r1 = x - h1.astype(_F32)
    h2 = r1.astype(_BF16)
    h3 = (r1 - h2.astype(_F32)).astype(_BF16)
    return (_dot(m, h3) + _dot(m, h2)) + _dot(m, h1)


def _hgrn_level_exponents(sums, cb_ref, c):
    cb = sums[:c]
    small = _hgrn_small_levels(c)
    args = []
    for h in _hgrn_levels(c):
        if h in small:
            i = 1 + small.index(h)
            args.append(sums[i * c:(i + 1) * c])
        elif h % V7X_SUBLANES == 0:
            pieces = []
            for j in range(c // (2 * h)):
                mid = cb_ref[pl.ds(j * 2 * h + h - 1, 1), :]
                pieces.append(mid - cb[j * 2 * h:j * 2 * h + h])
                pieces.append(cb[j * 2 * h + h:(j + 1) * 2 * h] - mid)
            args.append(jnp.concatenate(pieces, axis=0))
        else:
            mid_cb = jnp.concatenate(
                [jnp.broadcast_to(cb_ref[pl.ds(j * 2 * h + h - 1, 1), :], (2 * h, cb.shape[1]))
                 for j in range(c // (2 * h))], axis=0)
            args.append(-jnp.abs(cb - mid_cb))
    return args


def _pair_blockdiag(x):
    left = lax.broadcasted_iota(jnp.int32, x.shape, 1) < HG_KEY
    zero = jnp.zeros_like(x)
    return jnp.concatenate([jnp.where(left, x, zero), jnp.where(left, zero, x)], axis=0)


def _hgrn_prompt_kernel(hg_ref, lbg_ref, g_ref, m_ref, lvl_ref, o_ref, s_ref, st_scr, cb_scr):
    i = pl.program_id(1)
    c = HG_CHUNK
    pw = 2 * HG_KEY

    @pl.when(i == 0)
    def _():
        st_scr[...] = jnp.zeros(st_scr.shape, _F32)

    lb = _lower_bound(lbg_ref[...])
    lvl = lvl_ref[...]
    rr = lax.broadcasted_iota(jnp.int32, (pw, pw), 0) < HG_VAL
    cc = lax.broadcasted_iota(jnp.int32, (pw, pw), 1) < HG_KEY
    same_head = rr == cc
    pairs = range(HG_HEADS // 2)
    q, k, v, gate, sums, args = {}, {}, {}, {}, {}, {}
    for p in pairs:
        ps = slice(p * pw, (p + 1) * pw)
        q[p] = jax.nn.silu(hg_ref[:, p * pw:(p + 1) * pw])
        f = lb[:, ps] + (1.0 - lb[:, ps]) * jax.nn.sigmoid(
            hg_ref[:, HG_WIDTH + p * pw:HG_WIDTH + (p + 1) * pw])
        v[p] = hg_ref[:, 2 * HG_WIDTH + p * pw:2 * HG_WIDTH + (p + 1) * pw]
        gate[p] = jax.nn.silu(hg_ref[:, 3 * HG_WIDTH + p * pw:3 * HG_WIDTH + (p + 1) * pw])
        k[p] = 1.0 - f
        sums[p] = _exact_rows_matmul(m_ref[...], jnp.log(f) * _LOG2_E)
    for p in pairs:
        cb_scr[p] = sums[p][:c]
        args[p] = _hgrn_level_exponents(sums[p], cb_scr.at[p], c)
    a = {p: jnp.zeros((c, pw), _F32) for p in pairs}
    for li in range(len(_hgrn_levels(c))):
        for p in pairs:
            e = jnp.exp2(args[p][li])
            pr = _dot_nt((q[p] * e).astype(_BF16), _pair_blockdiag((k[p] * e).astype(_BF16)))
            a[p] = jnp.where(lvl == li, pr, a[p])
    o = {}
    for p in pairs:
        cb = sums[p][:c]
        total = cb[c - 1:c, :]
        vb = v[p].astype(_BF16)
        st = st_scr[p]
        o[p] = (_dot_nt((q[p] * jnp.exp2(cb)).astype(_BF16), st.astype(_BF16))
                + _dot(a[p].astype(_BF16), _pair_blockdiag(vb)))
        upd = _dot_tn(vb, (k[p] * jnp.exp2(total - cb)).astype(_BF16))
        st_scr[p] = st * jnp.exp2(total) + jnp.where(same_head, upd, 0.0)
    for p in pairs:
        qk = q[p] * k[p]
        for hh in range(2):
            hs = slice(hh * HG_KEY, (hh + 1) * HG_KEY)
            gs = slice(p * pw + hh * HG_KEY, p * pw + (hh + 1) * HG_KEY)
            oh = o[p][:, hs] + jnp.sum(qk[:, hs], axis=-1, keepdims=True) * v[p][:, hs]
            oh = oh * lax.rsqrt(jnp.mean(oh * oh, axis=-1, keepdims=True) + RMS_EPS)
            o_ref[:, gs] = oh * g_ref[:, gs] * gate[p][:, hs]

    @pl.when(i == pl.num_programs(1) - 1)
    def _():
        for p in range(HG_HEADS // 2):
            st = st_scr[p]
            s_ref[2 * p] = st[:HG_VAL, :HG_KEY].T
            s_ref[2 * p + 1] = st[HG_VAL:, HG_KEY:].T


def _hgrn_prompt(hg, lb_gamma, hg_g):
    bsz, l, _ = hg.shape
    c = HG_CHUNK
    mats, lvl = _hgrn_tables(c)
    return pl.pallas_call(
        _hgrn_prompt_kernel,
        grid=(bsz, l // c),
        in_specs=[
            pl.BlockSpec((None, c, HG_SLAB), lambda b, i: (b, i, 0)),
            pl.BlockSpec(lb_gamma.shape, lambda b, i: (0, 0)),
            pl.BlockSpec((1, HG_WIDTH), lambda b, i: (0, 0)),
            pl.BlockSpec(mats.shape, lambda b, i: (0, 0)),
            pl.BlockSpec(lvl.shape, lambda b, i: (0, 0)),
        ],
        out_specs=[
            pl.BlockSpec((None, c, HG_WIDTH), lambda b, i: (b, i, 0)),
            pl.BlockSpec((None, HG_HEADS, HG_KEY, HG_VAL), lambda b, i: (b, 0, 0, 0)),
        ],
        out_shape=[
            jax.ShapeDtypeStruct((bsz, l, HG_WIDTH), _F32),
            jax.ShapeDtypeStruct((bsz, HG_HEADS, HG_KEY, HG_VAL), _F32),
        ],
        scratch_shapes=[
            pltpu.VMEM((HG_HEADS // 2, 2 * HG_VAL, 2 * HG_KEY), _F32),
            pltpu.VMEM((HG_HEADS // 2, c, 2 * HG_KEY), _F32),
        ],
        compiler_params=pltpu.CompilerParams(
            dimension_semantics=("arbitrary", "arbitrary"), vmem_limit_bytes=V7X_VMEM_LIMIT_BYTES),
        name="hgrn_prompt",
    )(hg, lb_gamma, hg_g, jnp.asarray(mats, _BF16), jnp.asarray(lvl))


def _hgrn_sample_kernel(hg_ref, lbg_ref, g_ref, mask_ref, s0_ref, o_ref, s_ref):
    nb = SAMPLE_GROUP
    lb = _lower_bound(lbg_ref[...])
    q, cb, k, v, gate = [], [], [], [], []
    for t in range(DEC_SEQ):
        qt, lf, kt, vt, gt = _hgrn_gates(hg_ref[t], lb)
        q.append(qt); k.append(kt); v.append(vt); gate.append(gt)
        cb.append(lf if t == 0 else cb[-1] + lf)
    total = cb[-1]
    mask = mask_ref[...]
    for h in range(HG_HEADS):
        hs = slice(h * HG_KEY, (h + 1) * HG_KEY)
        qdec = jnp.concatenate([q[t][:, hs] * jnp.exp(cb[t][:, hs]) for t in range(DEC_SEQ)], axis=0)
        kend = jnp.concatenate(
            [k[t][:, hs] * jnp.exp(total[:, hs] - cb[t][:, hs]) for t in range(DEC_SEQ)], axis=0)
        vall = jnp.concatenate([v[t][:, hs] for t in range(DEC_SEQ)], axis=0).astype(_BF16)
        qblk = (jnp.concatenate([qdec] * nb, axis=1) * mask).astype(_BF16)
        kblk = (jnp.concatenate([kend] * nb, axis=1) * mask).astype(_BF16)
        s0 = s0_ref[:, h]
        o_inter = _dot(qblk, s0.reshape(nb * HG_KEY, HG_VAL).astype(_BF16))
        upd = _dot_tn(kblk, vall)
        decay_t = jnp.exp(total[:, hs]).T
        for b in range(nb):
            s_ref[b, h] = s0[b] * decay_t[:, b:b + 1] + upd[b * HG_KEY:(b + 1) * HG_KEY]
        for t in range(DEC_SEQ):
            o = o_inter[t * nb:(t + 1) * nb]
            for s in range(t + 1):
                w = q[t][:, hs] * k[s][:, hs]
                if s < t:
                    w = w * jnp.exp(cb[t][:, hs] - cb[s][:, hs])
                o = o + jnp.sum(w, axis=-1, keepdims=True) * v[s][:, hs]
            o = o * lax.rsqrt(jnp.mean(o * o, axis=-1, keepdims=True) + RMS_EPS)
            o_ref[t, :, hs] = o * g_ref[:, hs] * gate[t][:, hs]


def _hgrn_sample(hg, s0, lb_gamma, hg_g):
    l, db, _ = hg.shape
    nb = SAMPLE_GROUP
    rb = np.arange(l * nb)[:, None] % nb
    cbk = np.arange(nb * HG_KEY)[None, :] // HG_KEY
    mask = jnp.asarray((rb == cbk).astype(np.float32))
    return pl.pallas_call(
        _hgrn_sample_kernel,
        grid=(db // nb,),
        in_specs=[
            pl.BlockSpec((l, nb, HG_SLAB), lambda i: (0, i, 0)),
            pl.BlockSpec(lb_gamma.shape, lambda i: (0, 0)),
            pl.BlockSpec((1, HG_WIDTH), lambda i: (0, 0)),
            pl.BlockSpec(mask.shape, lambda i: (0, 0)),
            pl.BlockSpec((nb, HG_HEADS, HG_KEY, HG_VAL), lambda i: (i, 0, 0, 0)),
        ],
        out_specs=[
            pl.BlockSpec((l, nb, HG_WIDTH), lambda i: (0, i, 0)),
            pl.BlockSpec((nb, HG_HEADS, HG_KEY, HG_VAL), lambda i: (i, 0, 0, 0)),
        ],
        out_shape=[
            jax.ShapeDtypeStruct((l, db, HG_WIDTH), _F32),
            jax.ShapeDtypeStruct((db, HG_HEADS, HG_KEY, HG_VAL), _F32),
        ],
        compiler_params=pltpu.CompilerParams(
            dimension_semantics=("arbitrary",), vmem_limit_bytes=V7X_VMEM_LIMIT_BYTES),
        name="hgrn_sample",
    )(hg, lb_gamma, hg_g, mask, s0)


def _ffn_kernel(*refs, shift, blocks_per_seq, has_prev):
    if has_prev:
        (x_ref, oa_ref, oh_ref, ag_ref, wo_ref, n2_ref, wfi_ref, cw_ref, cb_ref, wfo_ref, fg_ref,
         prev_ref, y_ref, conv_ref, buf, hid) = refs
    else:
        (x_ref, oa_ref, oh_ref, ag_ref, wo_ref, n2_ref, wfi_ref, cw_ref, cb_ref, wfo_ref, fg_ref,
         y_ref, conv_ref, buf, hid) = refs
    tm = x_ref.shape[0]
    pad = buf.shape[0] - tm
    i = pl.program_id(0)

    oa = _rms(oa_ref[...], ag_ref[...])
    mix = jnp.concatenate([oa.astype(_BF16), oh_ref[...].astype(_BF16)], axis=-1)
    x1 = x_ref[...] + _dot(mix, wo_ref[...])
    h2 = _rms(x1, n2_ref[...]).astype(_BF16)

    if has_prev:
        buf[:pad, :] = prev_ref[...]
    else:
        @pl.when(i % blocks_per_seq == 0)
        def _():
            buf[:pad, :] = jnp.zeros((pad, D_FF), _F32)

        @pl.when(i % blocks_per_seq != 0)
        def _():
            buf[:pad, :] = buf[tm:tm + pad, :]

    for c in range(D_FF // FFN_COLS):
        cs = slice(c * FFN_COLS, (c + 1) * FFN_COLS)
        a = _dot(h2, wfi_ref[:, cs])
        gate = _dot(h2, wfi_ref[:, D_FF + c * FFN_COLS:D_FF + (c + 1) * FFN_COLS])
        buf[pad:, cs] = a
        ac = cb_ref[:, cs] + buf[pad - 2 * shift:pad - 2 * shift + tm, cs] * cw_ref[0:1, cs]
        ac = ac + buf[pad - shift:pad - shift + tm, cs] * cw_ref[1:2, cs]
        ac = ac + a * cw_ref[2:3, cs]
        hid[:, cs] = (jax.nn.silu(ac) * gate).astype(_BF16)
    y_ref[...] = _rms(x1 + _dot(hid[...], wfo_ref[...]), fg_ref[...])
    conv_ref[...] = buf[pad + tm - 2 * shift:, :]


def _ffn(x, oa, oh, attn_g, w_o, n2, wf_in, conv_w, conv_b, wf_out, final_g, *, shift,
         rows_per_seq, prev=None):
    t = x.shape[0]
    tm = FFN_ROWS
    nblk = t // tm
    blocks_per_seq = rows_per_seq // tm
    nseq = nblk // blocks_per_seq
    pad = max(V7X_SUBLANES, 2 * shift)
    tail = 2 * shift
    const = functools.partial(pl.BlockSpec, pipeline_mode=pl.Buffered(1))
    in_specs = [
        pl.BlockSpec((tm, D_MODEL), lambda i: (i, 0)),
        pl.BlockSpec((tm, ATT_WIDTH), lambda i: (i, 0)),
        pl.BlockSpec((tm, HG_WIDTH), lambda i: (i, 0)),
        const((1, ATT_WIDTH), lambda i: (0, 0)),
        const((ATT_WIDTH + HG_WIDTH, D_MODEL), lambda i: (0, 0)),
        const((1, D_MODEL), lambda i: (0, 0)),
        const((D_MODEL, 2 * D_FF), lambda i: (0, 0)),
        const((CONV_W, D_FF), lambda i: (0, 0)),
        const((1, D_FF), lambda i: (0, 0)),
        const((D_FF, D_MODEL), lambda i: (0, 0)),
        const((1, D_MODEL), lambda i: (0, 0)),
    ]
    args = [x, oa, oh, attn_g, w_o, n2, wf_in, conv_w, conv_b, wf_out, final_g]
    if prev is not None:
        in_specs.append(const((pad, D_FF), lambda i: (0, 0)))
        args.append(prev)
    return pl.pallas_call(
        functools.partial(_ffn_kernel, shift=shift, blocks_per_seq=blocks_per_seq,
                          has_prev=prev is not None),
        grid=(nblk,),
        in_specs=in_specs,
        out_specs=[
            pl.BlockSpec((tm, D_MODEL), lambda i: (i, 0)),
            pl.BlockSpec((None, tail, D_FF), lambda i: (i // blocks_per_seq, 0, 0)),
        ],
        out_shape=[
            jax.ShapeDtypeStruct((t, D_MODEL), _F32),
            jax.ShapeDtypeStruct((nseq, tail, D_FF), _F32),
        ],
        scratch_shapes=[pltpu.VMEM((pad + tm, D_FF), _F32), pltpu.VMEM((tm, D_FF), _BF16)],
        compiler_params=pltpu.CompilerParams(
            dimension_semantics=("arbitrary",), vmem_limit_bytes=V7X_VMEM_LIMIT_BYTES),
        name="ffn",
    )(*args)


def _interleave(streams):
    done = [0] * len(streams)
    live = set(range(len(streams)))
    while live:
        idx = min(live, key=lambda s: (done[s] + 1) / streams[s][1])
        try:
            next(streams[idx][0])
            done[idx] += 1
        except StopIteration:
            live.discard(idx)


def _attn_stream(q_ref, kvp_ref, kvc_ref, bias_scr, ag_ref, out_ref, seq_start):
    w = WINDOW
    pair = 2 * ATT_HEAD_DIM
    half_g = ATT_GROUP // 2
    rows = q_ref.shape[0]
    kvx = jnp.concatenate([kvp_ref[...], kvc_ref[...]], axis=0)
    kx = kvx[:, :pair]
    kr = pltpu.roll(kx, ATT_HEAD_DIM, 1).astype(_BF16)
    kx = kx.astype(_BF16)
    lo = lax.broadcasted_iota(jnp.int32, kx.shape, 1) < ATT_HEAD_DIM
    zero = jnp.zeros_like(kx)
    kmat = ((jnp.where(lo, kx, zero), jnp.where(lo, kr, zero)),
            (jnp.where(lo, zero, kr), jnp.where(lo, zero, kx)))
    v_t = kvx[:, pair:].T.astype(_BF16)
    row = lax.broadcasted_iota(jnp.int32, (2 * w, half_g * w), 0)
    first_key = lax.broadcasted_iota(jnp.int32, (2 * w, pair), 0) == 0
    first_col = lax.broadcasted_iota(jnp.int32, (ATT_HEAD_DIM, 2 * w), 1) == 0
    no_prev = (row >= 1) & (row < w) & seq_start
    units = [(kv, par) for kv in range(ATT_KV_HEADS) for par in range(2)]
    yield
    for j in range(rows // w):
        qs = slice(j * w, (j + 1) * w)
        ks = slice(j * w, (j + 2) * w)
        scores = {}
        for kv in range(ATT_KV_HEADS):
            base = kv * ATT_GROUP * ATT_HEAD_DIM
            lhs = jnp.concatenate(
                [q_ref[qs, base + i * pair:base + (i + 1) * pair] for i in range(half_g)], axis=0)
            lhs = (lhs * (ATT_HEAD_DIM ** -0.5)).astype(_BF16)
            for par in range(2):
                keys = jnp.where(first_key, jnp.zeros((), _BF16), kmat[par][kv][ks])
                scores[kv, par] = _dot_nt(keys, lhs)
        yield
        probs = {}
        for kv, par in units:
            s = scores[kv, par] + bias_scr[kv, par]
            if j == 0:
                s = jnp.where(no_prev, _NEG_INF, s)
            m = jnp.max(s, axis=0, keepdims=True)
            p = jnp.exp(s - m)
            probs[kv, par] = (p.astype(_BF16), jnp.sum(p, axis=0, keepdims=True))
        yield
        outs = {}
        for kv, par in units:
            p, den = probs[kv, par]
            vals = v_t[kv * ATT_HEAD_DIM:(kv + 1) * ATT_HEAD_DIM, ks]
            vals = jnp.where(first_col, jnp.zeros((), _BF16), vals)
            outs[kv, par] = _dot(vals, p) / den
        pieces = []
        for kv in range(ATT_KV_HEADS):
            for i in range(half_g):
                both = jnp.concatenate(
                    [outs[kv, par][:, i * w:(i + 1) * w] for par in range(2)], axis=0)
                pieces.append(both.T)
        o = jnp.concatenate(pieces, axis=-1)
        out_ref[qs, :ATT_WIDTH] = _rms(o, ag_ref[...]).astype(_BF16)
        yield


def _hgrn_stream(hg_ref, lbg_ref, g_ref, m_ref, lvl_ref, st_scr, cb_scr, out_ref, seq_start):
    c = HG_CHUNK
    pw = 2 * HG_KEY
    pairs = range(HG_HEADS // 2)
    lb = _lower_bound(lbg_ref[...])
    lvl = lvl_ref[...]
    rr = lax.broadcasted_iota(jnp.int32, (pw, pw), 0) < HG_VAL
    cc = lax.broadcasted_iota(jnp.int32, (pw, pw), 1) < HG_KEY
    same_head = rr == cc
    for ci in range(hg_ref.shape[0] // c):
        rs = slice(ci * c, (ci + 1) * c)
        q, k, v, gate, sums, args = {}, {}, {}, {}, {}, {}
        for p in pairs:
            ps = slice(p * pw, (p + 1) * pw)
            q[p] = jax.nn.silu(hg_ref[rs, p * pw:(p + 1) * pw])
            f = lb[:, ps] + (1.0 - lb[:, ps]) * jax.nn.sigmoid(
                hg_ref[rs, HG_WIDTH + p * pw:HG_WIDTH + (p + 1) * pw])
            v[p] = hg_ref[rs, 2 * HG_WIDTH + p * pw:2 * HG_WIDTH + (p + 1) * pw]
            gate[p] = jax.nn.silu(hg_ref[rs, 3 * HG_WIDTH + p * pw:3 * HG_WIDTH + (p + 1) * pw])
            k[p] = 1.0 - f
            sums[p] = _exact_rows_matmul(m_ref[...], jnp.log(f) * _LOG2_E)
        yield
        for p in pairs:
            cb_scr[p] = sums[p][:c]
            args[p] = _hgrn_level_exponents(sums[p], cb_scr.at[p], c)
        yield
        a = {p: jnp.zeros((c, pw), _F32) for p in pairs}
        for li in range(len(_hgrn_levels(c))):
            for p in pairs:
                e = jnp.exp2(args[p][li])
                pr = _dot_nt((q[p] * e).astype(_BF16), _pair_blockdiag((k[p] * e).astype(_BF16)))
                a[p] = jnp.where(lvl == li, pr, a[p])
            yield
        for p in pairs:
            cb = sums[p][:c]
            total = cb[c - 1:c, :]
            vb = v[p].astype(_BF16)
            st = st_scr[p]
            if ci == 0:
                st = jnp.where(seq_start, 0.0, st)
            o = (_dot_nt((q[p] * jnp.exp2(cb)).astype(_BF16), st.astype(_BF16))
                 + _dot(a[p].astype(_BF16), _pair_blockdiag(vb)))
            upd = _dot_tn(vb, (k[p] * jnp.exp2(total - cb)).astype(_BF16))
            st_scr[p] = st * jnp.exp2(total) + jnp.where(same_head, upd, 0.0)
            qk = q[p] * k[p]
            for hh in range(2):
                hs = slice(hh * HG_KEY, (hh + 1) * HG_KEY)
                gs = slice(p * pw + hh * HG_KEY, p * pw + (hh + 1) * HG_KEY)
                oh = o[:, hs] + jnp.sum(qk[:, hs], axis=-1, keepdims=True) * v[p][:, hs]
                oh = oh * lax.rsqrt(jnp.mean(oh * oh, axis=-1, keepdims=True) + RMS_EPS)
                out_ref[rs, ATT_WIDTH + gs.start:ATT_WIDTH + gs.stop] = (
                    oh * g_ref[:, gs] * gate[p][:, hs]).astype(_BF16)
        yield


def _ffn_stream(mix_ref, x_ref, wo_ref, n2_ref, wfi_ref, cw_ref, cb_ref, wfo_ref, fg_ref,
                y_ref, conv_ref, cbuf, carry, hid, seq_start):
    tm = x_ref.shape[0]
    pad = carry.shape[0]
    x1 = x_ref[...] + _dot(mix_ref[...], wo_ref[...])
    h2 = _rms(x1, n2_ref[...]).astype(_BF16)
    yield
    for c in range(D_FF // FFN_COLS):
        cs = slice(c * FFN_COLS, (c + 1) * FFN_COLS)
        buf = cbuf.at[c % 2]
        a = _dot(h2, wfi_ref[:, cs])
        gate = _dot(h2, wfi_ref[:, D_FF + c * FFN_COLS:D_FF + (c + 1) * FFN_COLS])
        buf[:pad, :] = jnp.where(seq_start, 0.0, carry[:, cs])
        buf[pad:, :] = a
        carry[:, cs] = a[tm - pad:, :]
        ac = cb_ref[:, cs] + buf[pad - 2:pad - 2 + tm, :] * cw_ref[0:1, cs]
        ac = ac + buf[pad - 1:pad - 1 + tm, :] * cw_ref[1:2, cs]
        ac = ac + a * cw_ref[2:3, cs]
        hid[:, cs] = (jax.nn.silu(ac) * gate).astype(_BF16)
        yield
    parts = []
    for n in range(D_MODEL // FFN_COLS):
        ns = slice(n * FFN_COLS, (n + 1) * FFN_COLS)
        parts.append(x1[:, ns] + _dot(hid[...], wfo_ref[:, ns]))
        yield
    y_ref[...] = _rms(jnp.concatenate(parts, axis=-1), fg_ref[...])
    conv_ref[...] = carry[pad - (CONV_W - 1):, :]


def _layer_kernel(relb_ref, sink_ref, bidx_ref, q_ref, kvp_ref, kvc_ref, hg_ref, lbg_ref, hgg_ref,
                  m_ref, lvl_ref, x_ref, ag_ref, wo_ref, n2_ref, wfi_ref, cw_ref, cb_ref, wfo_ref,
                  fg_ref, y_ref, conv_ref, s_ref, bias_scr, st_scr, cb_scr, mix_scr, cbuf, carry, hid,
                  *, blocks_per_seq):
    i = pl.program_id(0)
    nblk = pl.num_programs(0) - 1
    w = WINDOW

    @pl.when(i == 0)
    def _():
        bidx = bidx_ref[...]
        key0 = lax.broadcasted_iota(jnp.int32, bidx.shape, 0) == 0
        for kv in range(ATT_KV_HEADS):
            for par in range(2):
                for g in range(ATT_GROUP // 2):
                    h = kv * ATT_GROUP + 2 * g + par
                    tab = _bias_from_buckets(bidx, relb_ref, h)
                    bias_scr[kv, par, :, g * w:(g + 1) * w] = jnp.where(key0, sink_ref[h], tab)
        st_scr[...] = jnp.zeros(st_scr.shape, _F32)
        mix_scr[...] = jnp.zeros(mix_scr.shape, _BF16)
        carry[...] = jnp.zeros(carry.shape, _F32)

    slot = i % 2
    mix_start = (jnp.minimum(i, nblk - 1) % blocks_per_seq) == 0
    ffn_start = (jnp.maximum(i - 1, 0) % blocks_per_seq) == 0
    rows = q_ref.shape[0]
    chunks = rows // HG_CHUNK
    _interleave([
        (_ffn_stream(mix_scr.at[1 - slot], x_ref, wo_ref, n2_ref, wfi_ref, cw_ref, cb_ref, wfo_ref,
                     fg_ref, y_ref, conv_ref, cbuf, carry, hid, ffn_start),
         1 + D_FF // FFN_COLS + D_MODEL // FFN_COLS + 1),
        (_hgrn_stream(hg_ref, lbg_ref, hgg_ref, m_ref, lvl_ref, st_scr, cb_scr, mix_scr.at[slot],
                      mix_start), chunks * (3 + len(_hgrn_levels(HG_CHUNK)))),
        (_attn_stream(q_ref, kvp_ref, kvc_ref, bias_scr, ag_ref, mix_scr.at[slot], mix_start),
         1 + 3 * (rows // w)),
    ])

    @pl.when((i % blocks_per_seq == blocks_per_seq - 1) & (i < nblk))
    def _():
        for p in range(HG_HEADS // 2):
            st = st_scr[p]
            s_ref[2 * p] = st[:HG_VAL, :HG_KEY].T
            s_ref[2 * p + 1] = st[HG_VAL:, HG_KEY:].T


def _layer(x, att, hg, rel_bias, sinks, lb_gamma, hg_g, attn_g, w_o, n2, wf_in, conv_w, conv_b,
           wf_out, final_g, *, rows_per_seq):
    t = x.shape[0]
    rows = LAYER_ROWS
    nblk = t // rows
    bps = rows_per_seq // rows
    nseq = nblk // bps
    pad = V7X_SUBLANES
    qi = np.arange(WINDOW)[None, :]
    si = np.arange(2 * WINDOW)[:, None]
    bidx = jnp.asarray(_t5_bucket_np(WINDOW + qi - si))
    mats, lvl = _hgrn_tables(HG_CHUNK)
    kvcol = ATT_WIDTH // (2 * KV_WIDTH)
    wblocks = rows // WINDOW
    last = nblk - 1
    smem = pl.BlockSpec(memory_space=pltpu.SMEM)
    const = functools.partial(pl.BlockSpec, pipeline_mode=pl.Buffered(1))

    def mix_blk(i):
        return jnp.minimum(i, last)

    def ffn_blk(i):
        return jnp.maximum(i - 1, 0)

    in_specs = [
        smem, smem,
        const(bidx.shape, lambda i: (0, 0)),
        pl.BlockSpec((rows, ATT_WIDTH), lambda i: (mix_blk(i), 0)),
        pl.BlockSpec((WINDOW, 2 * KV_WIDTH),
                     lambda i: (jnp.maximum(mix_blk(i) * wblocks - 1, 0), kvcol)),
        pl.BlockSpec((rows, 2 * KV_WIDTH), lambda i: (mix_blk(i), kvcol)),
        pl.BlockSpec((rows, HG_SLAB), lambda i: (mix_blk(i), 0)),
        const(lb_gamma.shape, lambda i: (0, 0)),
        const((1, HG_WIDTH), lambda i: (0, 0)),
        const(mats.shape, lambda i: (0, 0)),
        const(lvl.shape, lambda i: (0, 0)),
        pl.BlockSpec((rows, D_MODEL), lambda i: (ffn_blk(i), 0)),
        const((1, ATT_WIDTH), lambda i: (0, 0)),
        const((ATT_WIDTH + HG_WIDTH, D_MODEL), lambda i: (0, 0)),
        const((1, D_MODEL), lambda i: (0, 0)),
        const((D_MODEL, 2 * D_FF), lambda i: (0, 0)),
        const((CONV_W, D_FF), lambda i: (0, 0)),
        const((1, D_FF), lambda i: (0, 0)),
        const((D_FF, D_MODEL), lambda i: (0, 0)),
        const((1, D_MODEL), lambda i: (0, 0)),
    ]
    return pl.pallas_call(
        functools.partial(_layer_kernel, blocks_per_seq=bps),
        grid=(nblk + 1,),
        in_specs=in_specs,
        out_specs=[
            pl.BlockSpec((rows, D_MODEL), lambda i: (ffn_blk(i), 0)),
            pl.BlockSpec((None, CONV_W - 1, D_FF), lambda i: (ffn_blk(i) // bps, 0, 0)),
            pl.BlockSpec((None, HG_HEADS, HG_KEY, HG_VAL), lambda i: (mix_blk(i) // bps, 0, 0, 0)),
        ],
        out_shape=[
            jax.ShapeDtypeStruct((t, D_MODEL), _F32),
            jax.ShapeDtypeStruct((nseq, CONV_W - 1, D_FF), _F32),
            jax.ShapeDtypeStruct((nseq, HG_HEADS, HG_KEY, HG_VAL), _F32),
        ],
        scratch_shapes=[
            pltpu.VMEM((ATT_KV_HEADS, 2, 2 * WINDOW, ATT_GROUP // 2 * WINDOW), _F32),
            pltpu.VMEM((HG_HEADS // 2, 2 * HG_VAL, 2 * HG_KEY), _F32),
            pltpu.VMEM((HG_HEADS // 2, HG_CHUNK, 2 * HG_KEY), _F32),
            pltpu.VMEM((2, rows, ATT_WIDTH + HG_WIDTH), _BF16),
            pltpu.VMEM((2, pad + rows, FFN_COLS), _F32),
            pltpu.VMEM((pad, D_FF), _F32),
            pltpu.VMEM((rows, D_FF), _BF16),
        ],
        compiler_params=pltpu.CompilerParams(
            dimension_semantics=("arbitrary",), vmem_limit_bytes=V7X_VMEM_LIMIT_BYTES),
        name="layer",
    )(rel_bias.reshape(-1), sinks, bidx, att, att, att, hg, lb_gamma, hg_g, jnp.asarray(mats, _BF16),
      jnp.asarray(lvl), x, attn_g, w_o, n2, wf_in, conv_w, conv_b, wf_out, final_g)


def kernel(x_prompt, x_sample, cache_k_win, cache_v_win, state_hgrn, state_conv, norm1_g, w_in,
           attn_sinks, rel_bias, lb_gamma, attn_out_g, hg_out_g, w_out, norm2_g, w_ffn_in, conv_w,
           conv_b, w_ffn_out, final_g):
    assert DEPTH == 1
    n1 = norm1_g[0][None]
    n2 = norm2_g[0][None]
    ag = attn_out_g[0][None]
    hgg = hg_out_g[0][None]
    fg = final_g[None]
    cb = conv_b[0][None]
    cw = conv_w[0]
    sinks = attn_sinks[0]
    w_in_b = w_in[0].astype(_BF16)
    w_o_b = w_out[0].astype(_BF16)
    wf_in_b = w_ffn_in[0].astype(_BF16)
    wf_out_b = w_ffn_out[0].astype(_BF16)
    ffn_w = (ag, w_o_b, n2, wf_in_b, cw, cb, wf_out_b, fg)

    xp = x_prompt.reshape(BATCH * SEQ, D_MODEL)
    att_p, hg_p = _proj(xp, n1, w_in_b)
    y_p, conv_p, s_p = _layer(xp, att_p, hg_p, rel_bias, sinks, lb_gamma, hgg, *ffn_w,
                              rows_per_seq=SEQ)
    kv_tail = att_p.reshape(BATCH, SEQ, ATT_SLAB)[:, SEQ - WINDOW:, ATT_WIDTH:]
    new_k_p = kv_tail[:, :, :KV_WIDTH].reshape(1, BATCH, WINDOW, ATT_KV_HEADS, ATT_HEAD_DIM)
    new_v_p = kv_tail[:, :, KV_WIDTH:].reshape(1, BATCH, WINDOW, ATT_KV_HEADS, ATT_HEAD_DIM)

    rows_s = DEC_SEQ * DEC_BATCH
    xs = x_sample.transpose(1, 0, 2).reshape(rows_s, D_MODEL)
    att_s, hg_s = _proj(xs, n1, w_in_b)
    att_s = att_s.reshape(DEC_SEQ, DEC_BATCH, ATT_SLAB)
    q_r = att_s[:, :, :ATT_WIDTH].reshape(DEC_SEQ, DEC_BATCH, ATT_KV_HEADS, ATT_GROUP, ATT_HEAD_DIM)
    q_r = q_r.transpose(1, 2, 3, 0, 4).reshape(DEC_BATCH, ATT_KV_HEADS, ATT_GROUP * DEC_SEQ, ATT_HEAD_DIM)
    q_zero = jnp.zeros_like(q_r[:, 0])
    q_r = jnp.stack([jnp.concatenate([q_r[:, 0], q_zero], axis=-1),
                     jnp.concatenate([q_zero, q_r[:, 1]], axis=-1)], axis=1)
    kv_new = att_s[:, :, ATT_WIDTH:].transpose(1, 0, 2)
    kv_new = jnp.pad(kv_new, ((0, 0), (0, SAMPLE_KEYS_PAD - SAMPLE_KEYS), (0, 0)))
    o16, new_k_s, new_v_s = _attn_sample(
        q_r, kv_new, cache_k_win[0].reshape(DEC_BATCH, WINDOW, KV_WIDTH),
        cache_v_win[0].reshape(DEC_BATCH, WINDOW, KV_WIDTH), rel_bias, sinks)
    oa_s = o16.reshape(DEC_BATCH, ATT_KV_HEADS, ATT_GROUP, DEC_SEQ, ATT_HEAD_DIM)
    oa_s = oa_s.transpose(3, 0, 1, 2, 4).reshape(rows_s, ATT_WIDTH)
    oh_s, s_s = _hgrn_sample(hg_s.reshape(DEC_SEQ, DEC_BATCH, HG_SLAB), state_hgrn[0], lb_gamma, hgg)
    prev = state_conv[0].transpose(1, 0, 2).reshape((CONV_W - 1) * DEC_BATCH, D_FF)
    y_s, conv_s = _ffn(xs, oa_s, oh_s.reshape(rows_s, HG_WIDTH), *ffn_w, shift=DEC_BATCH,
                       rows_per_seq=rows_s, prev=prev)
    y_s = y_s.reshape(DEC_SEQ, DEC_BATCH, D_MODEL).transpose(1, 0, 2)
    conv_s = conv_s.reshape(CONV_W - 1, DEC_BATCH, D_FF).transpose(1, 0, 2)

    cache_shape = (1, DEC_BATCH, WINDOW, ATT_KV_HEADS, ATT_HEAD_DIM)
    return (y_p.reshape(BATCH, SEQ, D_MODEL), y_s, new_k_p, new_v_p, s_p[None], conv_p[None],
            new_k_s.reshape(cache_shape), new_v_s.reshape(cache_shape), s_s[None], conv_s[None])
```

```python
import functools
import math

import jax
import jax.numpy as jnp
import numpy as np
from jax import lax
from jax.experimental import pallas as pl
from jax.experimental.pallas import tpu as pltpu

D_MODEL = 1024
BATCH = 2
SEQ = 8192
DEPTH = 1
DEC_BATCH = 128
DEC_SEQ = 4
ATT_HEAD_DIM = 64
ATT_HEADS = 8
ATT_KV_HEADS = 2
ATT_GROUP = 4
ATT_WIDTH = 512
KV_WIDTH = ATT_KV_HEADS * ATT_HEAD_DIM
WINDOW = 128
NUM_BUCKETS = 32
MAX_DISTANCE = 128
HG_KEY = 128
HG_VAL = 128
HG_HEADS = 4
HG_WIDTH = 512
D_FF = 2816
CONV_W = 3
RMS_EPS = 1e-6
ATT_SLAB = ATT_WIDTH + 2 * KV_WIDTH
HG_SLAB = 4 * HG_WIDTH
PROJ_WIDTH = ATT_SLAB + HG_SLAB

V7X_SUBLANES = 8
V7X_LANES = 128
V7X_MXU_DIM = 256
V7X_VMEM_LIMIT_BYTES = 60 * 1024 * 1024

PROJ_ROWS = 512
LAYER_ROWS = 512
FFN_COLS = V7X_MXU_DIM
HG_CHUNK = 128
SAMPLE_GROUP = 8
SAMPLE_KEYS = WINDOW + DEC_SEQ
SAMPLE_KEYS_PAD = SAMPLE_KEYS + (-SAMPLE_KEYS) % V7X_SUBLANES

_F32 = jnp.float32
_BF16 = jnp.bfloat16
_NEG_INF = float("-inf")
_LOG2_E = 1.0 / math.log(2.0)


def _dot(a, b):
    return jnp.dot(a, b, preferred_element_type=_F32)


def _dot_nt(a, b):
    return lax.dot_general(a, b, (((1,), (1,)), ((), ())), preferred_element_type=_F32)


def _dot_tn(a, b):
    return lax.dot_general(a, b, (((0,), (0,)), ((), ())), preferred_element_type=_F32)


def _rms(x, g):
    return x * lax.rsqrt(jnp.mean(x * x, axis=-1, keepdims=True) + RMS_EPS) * g


def _t5_bucket_np(dist):
    dist = np.asarray(dist)
    n = np.maximum(dist, 0)
    max_exact = NUM_BUCKETS // 2
    nf = np.maximum(n, 1).astype(np.float64)
    large = max_exact + (np.log(nf / max_exact) / math.log(MAX_DISTANCE / max_exact)
                         * (NUM_BUCKETS - max_exact)).astype(np.int32)
    large = np.minimum(large, NUM_BUCKETS - 1)
    bucket = np.where(n < max_exact, n, large)
    return np.where((dist >= 0) & (dist < WINDOW), bucket, -1).astype(np.int32)


def _bias_from_buckets(bidx, relb_ref, head):
    def body(b, acc):
        return jnp.where(bidx == b, relb_ref[b * ATT_HEADS + head], acc)
    return lax.fori_loop(0, NUM_BUCKETS, body, jnp.full(bidx.shape, _NEG_INF, _F32))


def _proj_kernel(x_ref, g_ref, w_ref, att_ref, hg_ref):
    h = _rms(x_ref[...], g_ref[...]).astype(_BF16)
    res = _dot(h, w_ref[...])
    att_ref[...] = res[:, :ATT_SLAB]
    hg_ref[...] = res[:, ATT_SLAB:]


def _proj(x, g, w):
    t = x.shape[0]
    tm = PROJ_ROWS
    return pl.pallas_call(
        _proj_kernel,
        grid=(t // tm,),
        in_specs=[
            pl.BlockSpec((tm, D_MODEL), lambda i: (i, 0)),
            pl.BlockSpec((1, D_MODEL), lambda i: (0, 0)),
            pl.BlockSpec((D_MODEL, PROJ_WIDTH), lambda i: (0, 0)),
        ],
        out_specs=[
            pl.BlockSpec((tm, ATT_SLAB), lambda i: (i, 0)),
            pl.BlockSpec((tm, HG_SLAB), lambda i: (i, 0)),
        ],
        out_shape=[
            jax.ShapeDtypeStruct((t, ATT_SLAB), _F32),
            jax.ShapeDtypeStruct((t, HG_SLAB), _F32),
        ],
        compiler_params=pltpu.CompilerParams(
            dimension_semantics=("arbitrary",), vmem_limit_bytes=V7X_VMEM_LIMIT_BYTES),
        name="proj",
    )(x, g, w)


def _proj_prompt_kernel(x_ref, g_ref, w_ref, att_ref, hg_ref, tail_ref):
    h = _rms(x_ref[...], g_ref[...]).astype(_BF16)
    res = _dot(h, w_ref[...])
    q = res[:, :ATT_WIDTH] * (ATT_HEAD_DIM ** -0.5)
    att_ref[...] = jnp.concatenate([q, res[:, ATT_WIDTH:ATT_SLAB]], axis=-1).astype(_BF16)
    hg_ref[...] = res[:, ATT_SLAB:]
    tail_ref[...] = res[x_ref.shape[0] - WINDOW:, ATT_WIDTH:ATT_SLAB]


def _proj_prompt(x, g, w, *, rows_per_seq):
    t = x.shape[0]
    tm = PROJ_ROWS
    bps = rows_per_seq // tm
    return pl.pallas_call(
        _proj_prompt_kernel,
        grid=(t // tm,),
        in_specs=[
            pl.BlockSpec((tm, D_MODEL), lambda i: (i, 0)),
            pl.BlockSpec((1, D_MODEL), lambda i: (0, 0)),
            pl.BlockSpec((D_MODEL, PROJ_WIDTH), lambda i: (0, 0)),
        ],
        out_specs=[
            pl.BlockSpec((tm, ATT_SLAB), lambda i: (i, 0)),
            pl.BlockSpec((tm, HG_SLAB), lambda i: (i, 0)),
            pl.BlockSpec((None, WINDOW, 2 * KV_WIDTH), lambda i: (i // bps, 0, 0)),
        ],
        out_shape=[
            jax.ShapeDtypeStruct((t, ATT_SLAB), _BF16),
            jax.ShapeDtypeStruct((t, HG_SLAB), _F32),
            jax.ShapeDtypeStruct((t // rows_per_seq, WINDOW, 2 * KV_WIDTH), _F32),
        ],
        compiler_params=pltpu.CompilerParams(
            dimension_semantics=("arbitrary",), vmem_limit_bytes=V7X_VMEM_LIMIT_BYTES),
        name="proj_prompt",
    )(x, g, w)


def _attn_sample_kernel(relb_ref, sink_ref, bidx_ref, q_ref, kvn_ref, kc_ref, vc_ref,
                        o_ref, nk_ref, nv_ref, bias_scr, sink_scr):
    @pl.when(pl.program_id(0) == 0)
    def _():
        bidx = bidx_ref[...]
        row_g = lax.broadcasted_iota(jnp.int32, (ATT_GROUP * DEC_SEQ, V7X_LANES), 0) // DEC_SEQ
        for kv in range(ATT_KV_HEADS):
            bias = jnp.full(bidx.shape, _NEG_INF, _F32)
            sk = jnp.zeros(row_g.shape, _F32)
            for g in range(ATT_GROUP):
                h = kv * ATT_GROUP + g
                rows = lax.broadcasted_iota(jnp.int32, bidx.shape, 0) // DEC_SEQ == g
                bias = jnp.where(rows, _bias_from_buckets(bidx, relb_ref, h), bias)
                sk = jnp.where(row_g == g, sink_ref[h], sk)
            bias_scr[kv] = bias
            sink_scr[kv] = sk

    keep = WINDOW - DEC_SEQ
    units = [(b, kv) for b in range(SAMPLE_GROUP) for kv in range(ATT_KV_HEADS)]
    scores = {}
    for b in range(SAMPLE_GROUP):
        kvn = kvn_ref[b]
        nk_ref[b, :keep, :] = kc_ref[b, DEC_SEQ:, :]
        nv_ref[b, :keep, :] = vc_ref[b, DEC_SEQ:, :]
        nk_ref[b, keep:, :] = kvn[:DEC_SEQ, :KV_WIDTH]
        nv_ref[b, keep:, :] = kvn[:DEC_SEQ, KV_WIDTH:]
        k = jnp.concatenate([kc_ref[b], kvn[:, :KV_WIDTH]], axis=0).astype(_BF16)
        for kv in range(ATT_KV_HEADS):
            q = (q_ref[b, kv] * (ATT_HEAD_DIM ** -0.5)).astype(_BF16)
            scores[b, kv] = _dot_nt(q, k)
    probs = {}
    for b, kv in units:
        s = scores[b, kv] + bias_scr[kv]
        sink = sink_scr[kv][:, :1]
        m = jnp.maximum(jnp.max(s, axis=-1, keepdims=True), sink)
        p = jnp.exp(s - m)
        den = jnp.sum(p, axis=-1, keepdims=True) + jnp.exp(sink - m)
        probs[b, kv] = (p.astype(_BF16), den)
    for b in range(SAMPLE_GROUP):
        v = jnp.concatenate([vc_ref[b], kvn_ref[b][:, KV_WIDTH:]], axis=0).astype(_BF16)
        for kv in range(ATT_KV_HEADS):
            p, den = probs[b, kv]
            o = _dot(p, v) / den
            o_ref[b, kv] = o[:, kv * ATT_HEAD_DIM:(kv + 1) * ATT_HEAD_DIM]


def _attn_sample(q_r, kv_new, cache_k, cache_v, rel_bias, sinks):
    db = q_r.shape[0]
    rows = ATT_GROUP * DEC_SEQ
    new_rows = SAMPLE_KEYS_PAD - WINDOW
    t = (np.arange(rows) % DEC_SEQ)[:, None]
    c = np.arange(SAMPLE_KEYS_PAD)[None, :]
    dist = np.where(c < SAMPLE_KEYS, t + WINDOW - c, -1)
    bidx = jnp.asarray(_t5_bucket_np(dist))
    smem = pl.BlockSpec(memory_space=pltpu.SMEM)
    g = SAMPLE_GROUP
    return pl.pallas_call(
        _attn_sample_kernel,
        grid=(db // g,),
        in_specs=[
            smem, smem,
            pl.BlockSpec((rows, SAMPLE_KEYS_PAD), lambda i: (0, 0)),
            pl.BlockSpec((g, ATT_KV_HEADS, rows, KV_WIDTH), lambda i: (i, 0, 0, 0)),
            pl.BlockSpec((g, new_rows, 2 * KV_WIDTH), lambda i: (i, 0, 0)),
            pl.BlockSpec((g, WINDOW, KV_WIDTH), lambda i: (i, 0, 0)),
            pl.BlockSpec((g, WINDOW, KV_WIDTH), lambda i: (i, 0, 0)),
        ],
        out_specs=[
            pl.BlockSpec((g, ATT_KV_HEADS, rows, ATT_HEAD_DIM), lambda i: (i, 0, 0, 0)),
            pl.BlockSpec((g, WINDOW, KV_WIDTH), lambda i: (i, 0, 0)),
            pl.BlockSpec((g, WINDOW, KV_WIDTH), lambda i: (i, 0, 0)),
        ],
        out_shape=[
            jax.ShapeDtypeStruct((db, ATT_KV_HEADS, rows, ATT_HEAD_DIM), _F32),
            jax.ShapeDtypeStruct((db, WINDOW, KV_WIDTH), _F32),
            jax.ShapeDtypeStruct((db, WINDOW, KV_WIDTH), _F32),
        ],
        scratch_shapes=[
            pltpu.VMEM((ATT_KV_HEADS, rows, SAMPLE_KEYS_PAD), _F32),
            pltpu.VMEM((ATT_KV_HEADS, rows, V7X_LANES), _F32),
        ],
        compiler_params=pltpu.CompilerParams(dimension_semantics=("arbitrary",)),
        name="attn_sample",
    )(rel_bias.reshape(-1), sinks, bidx, q_r, kv_new, cache_k, cache_v)


def _lower_bound(lbg):
    mx = jnp.max(lbg, axis=0, keepdims=True)
    e = jnp.exp(lbg - mx)
    return e[0:1] / jnp.sum(e, axis=0, keepdims=True)


def _hgrn_gates(hg, lb):
    q = jax.nn.silu(hg[:, 0:HG_WIDTH])
    f = lb + (1.0 - lb) * jax.nn.sigmoid(hg[:, HG_WIDTH:2 * HG_WIDTH])
    v = hg[:, 2 * HG_WIDTH:3 * HG_WIDTH]
    gate = jax.nn.silu(hg[:, 3 * HG_WIDTH:4 * HG_WIDTH])
    return q, jnp.log(f), 1.0 - f, v, gate


def _hgrn_levels(c):
    return [c >> (i + 1) for i in range(int(math.log2(c)))]


def _hgrn_small_levels(c):
    return [h for h in _hgrn_levels(c) if 2 * h < V7X_SUBLANES]


def _hgrn_tables(c):
    lvl = np.full((c, c), -1, np.int32)
    j = np.arange(c)[None, :]
    t = np.arange(c)[:, None]
    mats = [j <= t]
    for li, h in enumerate(_hgrn_levels(c)):
        mid = (t // (2 * h)) * (2 * h) + h
        same = (t // (2 * h)) == (j // (2 * h))
        lvl[same & (t >= mid) & (j < mid)] = li
        if h in _hgrn_small_levels(c):
            mats.append(np.where(t >= mid, (j >= mid) & (j <= t), (j > t) & (j < mid)))
    return np.concatenate(mats, axis=0).astype(np.float32), np.concatenate([lvl, lvl], axis=1)


def _pair_blockdiag(x):
    left = lax.broadcasted_iota(jnp.int32, x.shape, 1) < HG_KEY
    zero = jnp.zeros_like(x)
    return jnp.concatenate([jnp.where(left, x, zero), jnp.where(left, zero, x)], axis=0)


def _exact_rows_matmul_stacked(m3, x):
    h1 = x.astype(_BF16)
    r1 = x - h1.astype(_F32)
    h2 = r1.astype(_BF16)
    h3 = (r1 - h2.astype(_F32)).astype(_BF16)
    return _dot(m3, jnp.concatenate([h3, h2, h1], axis=0))


def _hgrn_level_rows(h, c):
    return [(slice(j * 2 * h, j * 2 * h + h), slice(j * 2 * h + h, (j + 1) * 2 * h))
            for j in range(c // (2 * h))]


def _hgrn_level_operands(q, k, sums, cb_ref, li, c):
    h = _hgrn_levels(c)[li]
    cb = sums[:c]
    small = _hgrn_small_levels(c)
    if h % V7X_SUBLANES == 0:
        lhs, rhs = [], []
        for j, (up, low) in enumerate(_hgrn_level_rows(h, c)):
            mid = cb_ref[pl.ds(j * 2 * h + h - 1, 1), :]
            lhs.append(q[low] * jnp.exp2(cb[low] - mid))
            rhs.append(k[up] * jnp.exp2(mid - cb[up]))
            rhs.append(jnp.zeros_like(k[low]))
        return (jnp.concatenate(lhs, axis=0).astype(_BF16),
                jnp.concatenate(rhs, axis=0).astype(_BF16), True)
    if h in small:
        i = 1 + small.index(h)
        arg = sums[i * c:(i + 1) * c]
    else:
        mid_cb = jnp.concatenate(
            [jnp.broadcast_to(cb_ref[pl.ds(j * 2 * h + h - 1, 1), :], (2 * h, cb.shape[1]))
             for j in range(c // (2 * h))], axis=0)
        arg = -jnp.abs(cb - mid_cb)
    e = jnp.exp2(arg)
    return (q * e).astype(_BF16), (k * e).astype(_BF16), False


def _hgrn_merge_level(a, pr, lvl, li, lower_only, c):
    if not lower_only:
        return jnp.where(lvl == li, pr, a)
    h = _hgrn_levels(c)[li]
    pieces = []
    for j, (up, low) in enumerate(_hgrn_level_rows(h, c)):
        pieces.append(a[up])
        pieces.append(jnp.where(lvl[low] == li, pr[j * h:(j + 1) * h], a[low]))
    return jnp.concatenate(pieces, axis=0)


def _hgrn_sample_kernel(hg_ref, lbg_ref, g_ref, mask_ref, s0_ref, o_ref, s_ref):
    nb = SAMPLE_GROUP
    lb = _lower_bound(lbg_ref[...])
    q, cb, k, v, gate = [], [], [], [], []
    for t in range(DEC_SEQ):
        qt, lf, kt, vt, gt = _hgrn_gates(hg_ref[t], lb)
        q.append(qt); k.append(kt); v.append(vt); gate.append(gt)
        cb.append(lf if t == 0 else cb[-1] + lf)
    total = cb[-1]
    mask = mask_ref[...]
    for h in range(HG_HEADS):
        hs = slice(h * HG_KEY, (h + 1) * HG_KEY)
        qdec = jnp.concatenate([q[t][:, hs] * jnp.exp(cb[t][:, hs]) for t in range(DEC_SEQ)], axis=0)
        kend = jnp.concatenate(
            [k[t][:, hs] * jnp.exp(total[:, hs] - cb[t][:, hs]) for t in range(DEC_SEQ)], axis=0)
        vall = jnp.concatenate([v[t][:, hs] for t in range(DEC_SEQ)], axis=0).astype(_BF16)
        qblk = (jnp.concatenate([qdec] * nb, axis=1) * mask).astype(_BF16)
        kblk = (jnp.concatenate([kend] * nb, axis=1) * mask).astype(_BF16)
        s0 = s0_ref[:, h]
        o_inter = _dot(qblk, s0.reshape(nb * HG_KEY, HG_VAL).astype(_BF16))
        upd = _dot_tn(kblk, vall)
        decay_t = jnp.exp(total[:, hs]).T
        for b in range(nb):
            s_ref[b, h] = s0[b] * decay_t[:, b:b + 1] + upd[b * HG_KEY:(b + 1) * HG_KEY]
        for t in range(DEC_SEQ):
            o = o_inter[t * nb:(t + 1) * nb]
            for s in range(t + 1):
                w = q[t][:, hs] * k[s][:, hs]
                if s < t:
                    w = w * jnp.exp(cb[t][:, hs] - cb[s][:, hs])
                o = o + jnp.sum(w, axis=-1, keepdims=True) * v[s][:, hs]
            o = o * lax.rsqrt(jnp.mean(o * o, axis=-1, keepdims=True) + RMS_EPS)
            o_ref[t, :, hs] = o * g_ref[:, hs] * gate[t][:, hs]


def _hgrn_sample(hg, s0, lb_gamma, hg_g):
    l, db, _ = hg.shape
    nb = SAMPLE_GROUP
    rb = np.arange(l * nb)[:, None] % nb
    cbk = np.arange(nb * HG_KEY)[None, :] // HG_KEY
    mask = jnp.asarray((rb == cbk).astype(np.float32))
    return pl.pallas_call(
        _hgrn_sample_kernel,
        grid=(db // nb,),
        in_specs=[
            pl.BlockSpec((l, nb, HG_SLAB), lambda i: (0, i, 0)),
            pl.BlockSpec(lb_gamma.shape, lambda i: (0, 0)),
            pl.BlockSpec((1, HG_WIDTH), lambda i: (0, 0)),
            pl.BlockSpec(mask.shape, lambda i: (0, 0)),
            pl.BlockSpec((nb, HG_HEADS, HG_KEY, HG_VAL), lambda i: (i, 0, 0, 0)),
        ],
        out_specs=[
            pl.BlockSpec((l, nb, HG_WIDTH), lambda i: (0, i, 0)),
            pl.BlockSpec((nb, HG_HEADS, HG_KEY, HG_VAL), lambda i: (i, 0, 0, 0)),
        ],
        out_shape=[
            jax.ShapeDtypeStruct((l, db, HG_WIDTH), _F32),
            jax.ShapeDtypeStruct((db, HG_HEADS, HG_KEY, HG_VAL), _F32),
        ],
        compiler_params=pltpu.CompilerParams(
            dimension_semantics=("arbitrary",), vmem_limit_bytes=V7X_VMEM_LIMIT_BYTES),
        name="hgrn_sample",
    )(hg, lb_gamma, hg_g, mask, s0)


def _col_chunks(n, wide):
    chunks = [(lo, wide) for lo in range(0, n - n % wide, wide)]
    if n % wide:
        chunks.append((n - n % wide, n % wide))
    return chunks


def _ffn_col_chunks(n):
    return _col_chunks(n, FFN_COLS)


def _ffn_sample_kernel(x_ref, oa_ref, oh_ref, ag_ref, wo_ref, n2_ref, wfi_ref, cw_ref, cb_ref, wfo_ref,
                       fg_ref, prev_ref, y_ref, conv_ref, buf, hid):
    tm = x_ref.shape[0]
    pad = buf.shape[0] - tm
    shift = pad // (CONV_W - 1)
    oa = _rms(oa_ref[...], ag_ref[...])
    mix = jnp.concatenate([oa.astype(_BF16), oh_ref[...].astype(_BF16)], axis=-1)
    x1 = x_ref[...] + _dot(mix, wo_ref[...])
    h2 = _rms(x1, n2_ref[...]).astype(_BF16)
    buf[:pad, :] = prev_ref[...]
    for lo, width in _ffn_col_chunks(D_FF):
        cs = slice(lo, lo + width)
        a = _dot(h2, wfi_ref[:, cs])
        gate = _dot(h2, wfi_ref[:, D_FF + lo:D_FF + lo + width])
        buf[pad:, cs] = a
        ac = cb_ref[:, cs] + buf[pad - 2 * shift:pad - 2 * shift + tm, cs] * cw_ref[0:1, cs]
        ac = ac + buf[pad - shift:pad - shift + tm, cs] * cw_ref[1:2, cs]
        ac = ac + a * cw_ref[2:3, cs]
        hid[:, cs] = (jax.nn.silu(ac) * gate).astype(_BF16)
    y_ref[...] = _rms(x1 + _dot(hid[...], wfo_ref[...]), fg_ref[...])
    conv_ref[...] = buf[tm:, :]


def _ffn_sample(x, oa, oh, attn_g, w_o, n2, wf_in, conv_w, conv_b, wf_out, final_g, prev):
    t = x.shape[0]
    pad = prev.shape[0]

    def full(a):
        return pl.BlockSpec(a.shape, lambda i: (0,) * a.ndim, pipeline_mode=pl.Buffered(1))

    args = (x, oa, oh, attn_g, w_o, n2, wf_in, conv_w, conv_b, wf_out, final_g, prev)
    return pl.pallas_call(
        _ffn_sample_kernel,
        grid=(1,),
        in_specs=[full(a) for a in args],
        out_specs=[pl.BlockSpec((t, D_MODEL), lambda i: (0, 0)),
                   pl.BlockSpec((pad, D_FF), lambda i: (0, 0))],
        out_shape=[jax.ShapeDtypeStruct((t, D_MODEL), _F32),
                   jax.ShapeDtypeStruct((pad, D_FF), _F32)],
        scratch_shapes=[pltpu.VMEM((pad + t, D_FF), _F32), pltpu.VMEM((t, D_FF), _BF16)],
        compiler_params=pltpu.CompilerParams(
            dimension_semantics=("arbitrary",), vmem_limit_bytes=V7X_VMEM_LIMIT_BYTES),
        name="ffn_sample",
    )(*args)


def _interleave(streams):
    done = [0] * len(streams)
    live = set(range(len(streams)))
    while live:
        idx = min(live, key=lambda s: (done[s] + 1) / streams[s][1])
        try:
            next(streams[idx][0])
            done[idx] += 1
        except StopIteration:
            live.discard(idx)


def _attn_stream(q_ref, kvp_ref, kvc_ref, bias_scr, ag_ref, out_ref, seq_start):
    w = WINDOW
    pair = 2 * ATT_HEAD_DIM
    half_g = ATT_GROUP // 2
    rows = q_ref.shape[0]
    kvx = jnp.concatenate([kvp_ref[...], kvc_ref[...]], axis=0)
    kx = kvx[:, :pair]
    kr = pltpu.roll(kx.astype(_F32), ATT_HEAD_DIM, 1).astype(_BF16)
    lo = lax.broadcasted_iota(jnp.int32, kx.shape, 1) < ATT_HEAD_DIM
    zero = jnp.zeros_like(kx)
    kmat = ((jnp.where(lo, kx, zero), jnp.where(lo, kr, zero)),
            (jnp.where(lo, zero, kr), jnp.where(lo, zero, kx)))
    v_t = kvx[:, pair:].astype(_F32).T.astype(_BF16)
    row = lax.broadcasted_iota(jnp.int32, (2 * w, half_g * w), 0)
    first_key = lax.broadcasted_iota(jnp.int32, (2 * w, pair), 0) == 0
    first_col = lax.broadcasted_iota(jnp.int32, (ATT_HEAD_DIM, 2 * w), 1) == 0
    no_prev = (row >= 1) & (row < w) & seq_start
    units = [(kv, par) for kv in range(ATT_KV_HEADS) for par in range(2)]
    yield
    for j in range(rows // w):
        qs = slice(j * w, (j + 1) * w)
        ks = slice(j * w, (j + 2) * w)
        scores = {}
        for kv in range(ATT_KV_HEADS):
            base = kv * ATT_GROUP * ATT_HEAD_DIM
            lhs = jnp.concatenate(
                [q_ref[qs, base + i * pair:base + (i + 1) * pair] for i in range(half_g)], axis=0)
            for par in range(2):
                keys = jnp.where(first_key, jnp.zeros((), _BF16), kmat[par][kv][ks])
                scores[kv, par] = _dot_nt(keys, lhs)
        yield
        probs = {}
        for kv, par in units:
            s = scores[kv, par] + bias_scr[kv, par]
            if j == 0:
                s = jnp.where(no_prev, _NEG_INF, s)
            m = jnp.max(s, axis=0, keepdims=True)
            p = jnp.exp(s - m)
            probs[kv, par] = (p.astype(_BF16), jnp.sum(p, axis=0, keepdims=True))
        yield
        outs = {}
        for kv, par in units:
            p, den = probs[kv, par]
            vals = v_t[kv * ATT_HEAD_DIM:(kv + 1) * ATT_HEAD_DIM, ks]
            vals = jnp.where(first_col, jnp.zeros((), _BF16), vals)
            outs[kv, par] = _dot(vals, p) / den
        pieces = []
        for kv in range(ATT_KV_HEADS):
            for i in range(half_g):
                both = jnp.concatenate(
                    [outs[kv, par][:, i * w:(i + 1) * w] for par in range(2)], axis=0)
                pieces.append(both.T)
        o = jnp.concatenate(pieces, axis=-1)
        out_ref[qs, :ATT_WIDTH] = _rms(o, ag_ref[...]).astype(_BF16)
        yield


def _hgrn_stream(hg_ref, lbg_ref, g_ref, m_ref, lvl_ref, st_scr, cb_scr, out_ref, seq_start):
    c = HG_CHUNK
    pw = 2 * HG_KEY
    pairs = range(HG_HEADS // 2)
    lb = _lower_bound(lbg_ref[...])
    lvl = lvl_ref[...]
    rr = lax.broadcasted_iota(jnp.int32, (pw, pw), 0) < HG_VAL
    cc = lax.broadcasted_iota(jnp.int32, (pw, pw), 1) < HG_KEY
    same_head = rr == cc
    for ci in range(hg_ref.shape[0] // c):
        rs = slice(ci * c, (ci + 1) * c)
        q, k, v, gate, sums = {}, {}, {}, {}, {}
        for p in pairs:
            ps = slice(p * pw, (p + 1) * pw)
            q[p] = jax.nn.silu(hg_ref[rs, p * pw:(p + 1) * pw])
            f = lb[:, ps] + (1.0 - lb[:, ps]) * jax.nn.sigmoid(
                hg_ref[rs, HG_WIDTH + p * pw:HG_WIDTH + (p + 1) * pw])
            v[p] = hg_ref[rs, 2 * HG_WIDTH + p * pw:2 * HG_WIDTH + (p + 1) * pw]
            gate[p] = jax.nn.silu(hg_ref[rs, 3 * HG_WIDTH + p * pw:3 * HG_WIDTH + (p + 1) * pw])
            k[p] = 1.0 - f
            sums[p] = _exact_rows_matmul_stacked(m_ref[...], jnp.log(f) * _LOG2_E)
        yield
        for p in pairs:
            cb_scr[p] = sums[p][:c]
        levels = _hgrn_levels(c)
        a = {p: jnp.zeros((c, pw), _F32) for p in pairs}
        ops = {p: _hgrn_level_operands(q[p], k[p], sums[p], cb_scr.at[p], 0, c) for p in pairs}
        yield
        for li in range(len(levels)):
            nxt = None
            if li + 1 < len(levels):
                nxt = {p: _hgrn_level_operands(q[p], k[p], sums[p], cb_scr.at[p], li + 1, c)
                       for p in pairs}
            for p in pairs:
                lhs, rhs, low = ops[p]
                pr = _dot_nt(lhs, _pair_blockdiag(rhs))
                a[p] = _hgrn_merge_level(a[p], pr, lvl, li, low, c)
            ops = nxt
            yield
        for p in pairs:
            cb = sums[p][:c]
            total = cb[c - 1:c, :]
            vb = v[p].astype(_BF16)
            st = st_scr[p]
            if ci == 0:
                st = jnp.where(seq_start, 0.0, st)
            o = (_dot_nt((q[p] * jnp.exp2(cb)).astype(_BF16), st.astype(_BF16))
                 + _dot(a[p].astype(_BF16), _pair_blockdiag(vb)))
            upd = _dot_tn(vb, (k[p] * jnp.exp2(total - cb)).astype(_BF16))
            st_scr[p] = st * jnp.exp2(total) + jnp.where(same_head, upd, 0.0)
            qk = q[p] * k[p]
            for hh in range(2):
                hs = slice(hh * HG_KEY, (hh + 1) * HG_KEY)
                gs = slice(p * pw + hh * HG_KEY, p * pw + (hh + 1) * HG_KEY)
                oh = o[:, hs] + jnp.sum(qk[:, hs], axis=-1, keepdims=True) * v[p][:, hs]
                oh = oh * lax.rsqrt(jnp.mean(oh * oh, axis=-1, keepdims=True) + RMS_EPS)
                out_ref[rs, ATT_WIDTH + gs.start:ATT_WIDTH + gs.stop] = (
                    oh * g_ref[:, gs] * gate[p][:, hs]).astype(_BF16)
        yield


def _ffn_stream(mix_ref, x_ref, wo_ref, n2_ref, wfi_ref, cw_ref, cb_ref, wfo_ref, fg_ref,
                y_ref, conv_ref, cbuf, carry, hid, seq_start):
    tm = x_ref.shape[0]
    pad = carry.shape[0]
    x1 = x_ref[...] + _dot(mix_ref[...], wo_ref[...])
    y_ref[...] = x1
    h2 = _rms(x1, n2_ref[...]).astype(_BF16)
    yield
    for c, (lo, width) in enumerate(_ffn_col_chunks(D_FF)):
        cs = slice(lo, lo + width)
        buf = cbuf.at[c % 2]
        a = _dot(h2, wfi_ref[:, cs])
        gate = _dot(h2, wfi_ref[:, D_FF + lo:D_FF + lo + width])
        buf[:pad, :width] = jnp.where(seq_start, 0.0, carry[:, cs])
        buf[pad:, :width] = a
        carry[:, cs] = a[tm - pad:, :]
        ac = cb_ref[:, cs] + buf[pad - 2:pad - 2 + tm, :width] * cw_ref[0:1, cs]
        ac = ac + buf[pad - 1:pad - 1 + tm, :width] * cw_ref[1:2, cs]
        ac = ac + a * cw_ref[2:3, cs]
        hid[:, cs] = (jax.nn.silu(ac) * gate).astype(_BF16)
        yield
    conv_ref[...] = carry[pad - (CONV_W - 1):, :]
    for lo, width in _ffn_col_chunks(D_MODEL):
        ns = slice(lo, lo + width)
        y_ref[:, ns] = y_ref[:, ns] + _dot(hid[...], wfo_ref[:, ns])
        yield
    y_ref[...] = _rms(y_ref[...], fg_ref[...])


def _layer_kernel(relb_ref, sink_ref, bidx_ref, q_ref, kvp_ref, kvc_ref, hg_ref, lbg_ref, hgg_ref,
                  m_ref, lvl_ref, x_ref, ag_ref, wo_ref, n2_ref, wfi_ref, cw_ref, cb_ref, wfo_ref,
                  fg_ref, y_ref, conv_ref, s_ref, bias_scr, st_scr, cb_scr, mix_scr, cbuf, carry, hid,
                  *, blocks_per_seq):
    i = pl.program_id(0)
    nblk = pl.num_programs(0) - 1
    w = WINDOW

    @pl.when(i == 0)
    def _():
        bidx = bidx_ref[...]
        key0 = lax.broadcasted_iota(jnp.int32, bidx.shape, 0) == 0
        for kv in range(ATT_KV_HEADS):
            for par in range(2):
                for g in range(ATT_GROUP // 2):
                    h = kv * ATT_GROUP + 2 * g + par
                    tab = _bias_from_buckets(bidx, relb_ref, h)
                    bias_scr[kv, par, :, g * w:(g + 1) * w] = jnp.where(key0, sink_ref[h], tab)
        st_scr[...] = jnp.zeros(st_scr.shape, _F32)
        mix_scr[...] = jnp.zeros(mix_scr.shape, _BF16)
        carry[...] = jnp.zeros(carry.shape, _F32)

    slot = i % 2
    mix_start = (jnp.minimum(i, nblk - 1) % blocks_per_seq) == 0
    ffn_start = (jnp.maximum(i - 1, 0) % blocks_per_seq) == 0
    rows = q_ref.shape[0]
    chunks = rows // HG_CHUNK
    _interleave([
        (_ffn_stream(mix_scr.at[1 - slot], x_ref, wo_ref, n2_ref, wfi_ref, cw_ref, cb_ref, wfo_ref,
                     fg_ref, y_ref, conv_ref, cbuf, carry, hid, ffn_start),
         1 + len(_ffn_col_chunks(D_FF)) + len(_ffn_col_chunks(D_MODEL)) + 1),
        (_hgrn_stream(hg_ref, lbg_ref, hgg_ref, m_ref, lvl_ref, st_scr, cb_scr, mix_scr.at[slot],
                      mix_start), chunks * (3 + len(_hgrn_levels(HG_CHUNK)))),
        (_attn_stream(q_ref, kvp_ref, kvc_ref, bias_scr, ag_ref, mix_scr.at[slot], mix_start),
         1 + 3 * (rows // w)),
    ])

    @pl.when((i % blocks_per_seq == blocks_per_seq - 1) & (i < nblk))
    def _():
        for p in range(HG_HEADS // 2):
            st = st_scr[p]
            s_ref[2 * p] = st[:HG_VAL, :HG_KEY].T
            s_ref[2 * p + 1] = st[HG_VAL:, HG_KEY:].T


def _layer(x, att, hg, rel_bias, sinks, lb_gamma, hg_g, attn_g, w_o, n2, wf_in, conv_w, conv_b,
           wf_out, final_g, *, rows_per_seq):
    t = x.shape[0]
    rows = LAYER_ROWS
    nblk = t // rows
    bps = rows_per_seq // rows
    nseq = nblk // bps
    pad = V7X_SUBLANES
    qi = np.arange(WINDOW)[None, :]
    si = np.arange(2 * WINDOW)[:, None]
    bidx = jnp.asarray(_t5_bucket_np(WINDOW + qi - si))
    mats, lvl = _hgrn_tables(HG_CHUNK)
    mats = np.tile(mats, (1, 3))
    kvcol = ATT_WIDTH // (2 * KV_WIDTH)
    wblocks = rows // WINDOW
    last = nblk - 1
    smem = pl.BlockSpec(memory_space=pltpu.SMEM)
    const = functools.partial(pl.BlockSpec, pipeline_mode=pl.Buffered(1))

    def mix_blk(i):
        return jnp.minimum(i, last)

    def ffn_blk(i):
        return jnp.maximum(i - 1, 0)

    in_specs = [
        smem, smem,
        const(bidx.shape, lambda i: (0, 0)),
        pl.BlockSpec((rows, ATT_WIDTH), lambda i: (mix_blk(i), 0)),
        pl.BlockSpec((WINDOW, 2 * KV_WIDTH),
                     lambda i: (jnp.maximum(mix_blk(i) * wblocks - 1, 0), kvcol)),
        pl.BlockSpec((rows, 2 * KV_WIDTH), lambda i: (mix_blk(i), kvcol)),
        pl.BlockSpec((rows, HG_SLAB), lambda i: (mix_blk(i), 0)),
        const(lb_gamma.shape, lambda i: (0, 0)),
        const((1, HG_WIDTH), lambda i: (0, 0)),
        const(mats.shape, lambda i: (0, 0)),
        const(lvl.shape, lambda i: (0, 0)),
        pl.BlockSpec((rows, D_MODEL), lambda i: (ffn_blk(i), 0)),
        const((1, ATT_WIDTH), lambda i: (0, 0)),
        const((ATT_WIDTH + HG_WIDTH, D_MODEL), lambda i: (0, 0)),
        const((1, D_MODEL), lambda i: (0, 0)),
        const((D_MODEL, 2 * D_FF), lambda i: (0, 0)),
        const((CONV_W, D_FF), lambda i: (0, 0)),
        const((1, D_FF), lambda i: (0, 0)),
        const((D_FF, D_MODEL), lambda i: (0, 0)),
        const((1, D_MODEL), lambda i: (0, 0)),
    ]
    return pl.pallas_call(
        functools.partial(_layer_kernel, blocks_per_seq=bps),
        grid=(nblk + 1,),
        in_specs=in_specs,
        out_specs=[
            pl.BlockSpec((rows, D_MODEL), lambda i: (ffn_blk(i), 0)),
            pl.BlockSpec((None, CONV_W - 1, D_FF), lambda i: (ffn_blk(i) // bps, 0, 0)),
            pl.BlockSpec((None, HG_HEADS, HG_KEY, HG_VAL), lambda i: (mix_blk(i) // bps, 0, 0, 0)),
        ],
        out_shape=[
            jax.ShapeDtypeStruct((t, D_MODEL), _F32),
            jax.ShapeDtypeStruct((nseq, CONV_W - 1, D_FF), _F32),
            jax.ShapeDtypeStruct((nseq, HG_HEADS, HG_KEY, HG_VAL), _F32),
        ],
        scratch_shapes=[
            pltpu.VMEM((ATT_KV_HEADS, 2, 2 * WINDOW, ATT_GROUP // 2 * WINDOW), _F32),
            pltpu.VMEM((HG_HEADS // 2, 2 * HG_VAL, 2 * HG_KEY), _F32),
            pltpu.VMEM((HG_HEADS // 2, HG_CHUNK, 2 * HG_KEY), _F32),
            pltpu.VMEM((2, rows, ATT_WIDTH + HG_WIDTH), _BF16),
            pltpu.VMEM((2, pad + rows, FFN_COLS), _F32),
            pltpu.VMEM((pad, D_FF), _F32),
            pltpu.VMEM((rows, D_FF), _BF16),
        ],
        compiler_params=pltpu.CompilerParams(
            dimension_semantics=("arbitrary",), vmem_limit_bytes=V7X_VMEM_LIMIT_BYTES),
        name="layer",
    )(rel_bias.reshape(-1), sinks, bidx, att, att, att, hg, lb_gamma, hg_g, jnp.asarray(mats, _BF16),
      jnp.asarray(lvl), x, attn_g, w_o, n2, wf_in, conv_w, conv_b, wf_out, final_g)


def _proj_stream(x_ref, g_ref, w_ref, slab_ref, tail_ref):
    rows = x_ref.shape[0]
    h = _rms(x_ref[...], g_ref[...]).astype(_BF16)
    yield
    for lo, width in _col_chunks(PROJ_WIDTH, 2 * V7X_MXU_DIM):
        slab_ref[:, lo:lo + width] = _dot(h, w_ref[:, lo:lo + width])
        yield
    tail_ref[...] = slab_ref[rows - WINDOW:, ATT_WIDTH:ATT_SLAB]


def _mixer_kernel(relb_ref, sink_ref, bidx_ref, x_ref, g_ref, w_ref, lbg_ref, hgg_ref, m_ref, lvl_ref,
                  ag_ref, mix_ref, kvt_ref, s_ref, bias_scr, st_scr, cb_scr, slab_scr, tail_scr, *,
                  blocks_per_seq):
    i = pl.program_id(0)
    w = WINDOW

    @pl.when(i == 0)
    def _():
        bidx = bidx_ref[...]
        key0 = lax.broadcasted_iota(jnp.int32, bidx.shape, 0) == 0
        for kv in range(ATT_KV_HEADS):
            for par in range(2):
                for g in range(ATT_GROUP // 2):
                    h = kv * ATT_GROUP + 2 * g + par
                    tab = _bias_from_buckets(bidx, relb_ref, h)
                    bias_scr[kv, par, :, g * w:(g + 1) * w] = jnp.where(key0, sink_ref[h], tab)
        st_scr[...] = jnp.zeros(st_scr.shape, _F32)
        slab_scr[...] = jnp.zeros(slab_scr.shape, _F32)
        tail_scr[...] = jnp.zeros(tail_scr.shape, _F32)

    slot = i % 2
    done = jnp.maximum(i - 1, 0)
    seq_start = (done % blocks_per_seq) == 0
    slab = slab_scr.at[1 - slot]
    rows = x_ref.shape[0]
    _interleave([
        (_proj_stream(x_ref, g_ref, w_ref, slab_scr.at[slot], tail_scr.at[i % 3]),
         2 + len(_col_chunks(PROJ_WIDTH, 2 * V7X_MXU_DIM))),
        (_hgrn_stream(slab.at[:, ATT_SLAB:], lbg_ref, hgg_ref, m_ref, lvl_ref, st_scr, cb_scr, mix_ref,
                      seq_start), (rows // HG_CHUNK) * (3 + len(_hgrn_levels(HG_CHUNK)))),
        (_attn_stream(slab.at[:, :ATT_WIDTH], tail_scr.at[(i + 1) % 3], slab.at[:, ATT_WIDTH:ATT_SLAB],
                      bias_scr, ag_ref, mix_ref, seq_start), 1 + 3 * (rows // w)),
    ])
    kvt_ref[...] = tail_scr[(i + 2) % 3]

    @pl.when((i >= 1) & (done % blocks_per_seq == blocks_per_seq - 1))
    def _():
        for p in range(HG_HEADS // 2):
            st = st_scr[p]
            s_ref[2 * p] = st[:HG_VAL, :HG_KEY].T
            s_ref[2 * p + 1] = st[HG_VAL:, HG_KEY:].T


def _mixer(x, n1, w_in, rel_bias, sinks, lb_gamma, hg_g, attn_g, *, rows_per_seq):
    t = x.shape[0]
    rows = LAYER_ROWS
    nblk = t // rows
    bps = rows_per_seq // rows
    nseq = nblk // bps
    qi = np.arange(WINDOW)[None, :]
    si = np.arange(2 * WINDOW)[:, None]
    bidx = jnp.asarray(_t5_bucket_np(WINDOW + qi - si))
    mats, lvl = _hgrn_tables(HG_CHUNK)
    mats = np.tile(mats, (1, 3))
    last = nblk - 1
    smem = pl.BlockSpec(memory_space=pltpu.SMEM)
    const = functools.partial(pl.BlockSpec, pipeline_mode=pl.Buffered(1))

    def done_blk(i):
        return jnp.maximum(i - 1, 0)

    return pl.pallas_call(
        functools.partial(_mixer_kernel, blocks_per_seq=bps),
        grid=(nblk + 1,),
        in_specs=[
            smem, smem,
            const(bidx.shape, lambda i: (0, 0)),
            pl.BlockSpec((rows, D_MODEL), lambda i: (jnp.minimum(i, last), 0)),
            const((1, D_MODEL), lambda i: (0, 0)),
            const((D_MODEL, PROJ_WIDTH), lambda i: (0, 0)),
            const(lb_gamma.shape, lambda i: (0, 0)),
            const((1, HG_WIDTH), lambda i: (0, 0)),
            const(mats.shape, lambda i: (0, 0)),
            const(lvl.shape, lambda i: (0, 0)),
            const((1, ATT_WIDTH), lambda i: (0, 0)),
        ],
        out_specs=[
            pl.BlockSpec((rows, ATT_WIDTH + HG_WIDTH), lambda i: (done_blk(i), 0)),
            pl.BlockSpec((None, WINDOW, 2 * KV_WIDTH), lambda i: (done_blk(i) // bps, 0, 0)),
            pl.BlockSpec((None, HG_HEADS, HG_KEY, HG_VAL), lambda i: (done_blk(i) // bps, 0, 0, 0)),
        ],
        out_shape=[
            jax.ShapeDtypeStruct((t, ATT_WIDTH + HG_WIDTH), _BF16),
            jax.ShapeDtypeStruct((nseq, WINDOW, 2 * KV_WIDTH), _F32),
            jax.ShapeDtypeStruct((nseq, HG_HEADS, HG_KEY, HG_VAL), _F32),
        ],
        scratch_shapes=[
            pltpu.VMEM((ATT_KV_HEADS, 2, 2 * WINDOW, ATT_GROUP // 2 * WINDOW), _F32),
            pltpu.VMEM((HG_HEADS // 2, 2 * HG_VAL, 2 * HG_KEY), _F32),
            pltpu.VMEM((HG_HEADS // 2, HG_CHUNK, 2 * HG_KEY), _F32),
            pltpu.VMEM((2, rows, PROJ_WIDTH), _F32),
            pltpu.VMEM((3, WINDOW, 2 * KV_WIDTH), _F32),
        ],
        compiler_params=pltpu.CompilerParams(
            dimension_semantics=("arbitrary",), vmem_limit_bytes=V7X_VMEM_LIMIT_BYTES),
        name="mixer",
    )(rel_bias.reshape(-1), sinks, bidx, x, n1, w_in, lb_gamma, hg_g, jnp.asarray(mats, _BF16),
      jnp.asarray(lvl), attn_g)


def _ffn_prompt_kernel(x_ref, mix_ref, wo_ref, n2_ref, wfi_ref, cw_ref, cb_ref, wfo_ref, fg_ref,
                       y_ref, conv_ref, cbuf, carry, hid, *, blocks_per_seq):
    i = pl.program_id(0)

    @pl.when(i == 0)
    def _():
        carry[...] = jnp.zeros(carry.shape, _F32)

    for _ in _ffn_stream(mix_ref, x_ref, wo_ref, n2_ref, wfi_ref, cw_ref, cb_ref, wfo_ref, fg_ref,
                         y_ref, conv_ref, cbuf, carry, hid, (i % blocks_per_seq) == 0):
        pass


def _ffn_prompt(x, mix, w_o, n2, wf_in, conv_w, conv_b, wf_out, final_g, *, rows_per_seq):
    t = x.shape[0]
    rows = LAYER_ROWS
    nblk = t // rows
    bps = rows_per_seq // rows
    pad = V7X_SUBLANES
    const = functools.partial(pl.BlockSpec, pipeline_mode=pl.Buffered(1))
    return pl.pallas_call(
        functools.partial(_ffn_prompt_kernel, blocks_per_seq=bps),
        grid=(nblk,),
        in_specs=[
            pl.BlockSpec((rows, D_MODEL), lambda i: (i, 0)),
            pl.BlockSpec((rows, ATT_WIDTH + HG_WIDTH), lambda i: (i, 0)),
            const((ATT_WIDTH + HG_WIDTH, D_MODEL), lambda i: (0, 0)),
            const((1, D_MODEL), lambda i: (0, 0)),
            const((D_MODEL, 2 * D_FF), lambda i: (0, 0)),
            const((CONV_W, D_FF), lambda i: (0, 0)),
            const((1, D_FF), lambda i: (0, 0)),
            const((D_FF, D_MODEL), lambda i: (0, 0)),
            const((1, D_MODEL), lambda i: (0, 0)),
        ],
        out_specs=[
            pl.BlockSpec((rows, D_MODEL), lambda i: (i, 0)),
            pl.BlockSpec((None, CONV_W - 1, D_FF), lambda i: (i // bps, 0, 0)),
        ],
        out_shape=[
            jax.ShapeDtypeStruct((t, D_MODEL), _F32),
            jax.ShapeDtypeStruct((nblk // bps, CONV_W - 1, D_FF), _F32),
        ],
        scratch_shapes=[
            pltpu.VMEM((2, pad + rows, FFN_COLS), _F32),
            pltpu.VMEM((pad, D_FF), _F32),
            pltpu.VMEM((rows, D_FF), _BF16),
        ],
        compiler_params=pltpu.CompilerParams(
            dimension_semantics=("arbitrary",), vmem_limit_bytes=V7X_VMEM_LIMIT_BYTES),
        name="ffn_prompt",
    )(x, mix, w_o, n2, wf_in, conv_w, conv_b, wf_out, final_g)


def kernel(x_prompt, x_sample, cache_k_win, cache_v_win, state_hgrn, state_conv, norm1_g, w_in,
           attn_sinks, rel_bias, lb_gamma, attn_out_g, hg_out_g, w_out, norm2_g, w_ffn_in, conv_w,
           conv_b, w_ffn_out, final_g):
    assert DEPTH == 1
    n1 = norm1_g[0][None]
    n2 = norm2_g[0][None]
    ag = attn_out_g[0][None]
    hgg = hg_out_g[0][None]
    fg = final_g[None]
    cb = conv_b[0][None]
    cw = conv_w[0]
    sinks = attn_sinks[0]
    w_in_b = w_in[0].astype(_BF16)
    w_o_b = w_out[0].astype(_BF16)
    wf_in_b = w_ffn_in[0].astype(_BF16)
    wf_out_b = w_ffn_out[0].astype(_BF16)
    ffn_w = (ag, w_o_b, n2, wf_in_b, cw, cb, wf_out_b, fg)

    xp = x_prompt.reshape(BATCH * SEQ, D_MODEL)
    att_p, hg_p, kv_tail = _proj_prompt(xp, n1, w_in_b, rows_per_seq=SEQ)
    y_p, conv_p, s_p = _layer(xp, att_p, hg_p, rel_bias, sinks, lb_gamma, hgg, *ffn_w,
                              rows_per_seq=SEQ)
    new_k_p = kv_tail[:, :, :KV_WIDTH].reshape(1, BATCH, WINDOW, ATT_KV_HEADS, ATT_HEAD_DIM)
    new_v_p = kv_tail[:, :, KV_WIDTH:].reshape(1, BATCH, WINDOW, ATT_KV_HEADS, ATT_HEAD_DIM)

    rows_s = DEC_SEQ * DEC_BATCH
    xs = x_sample.transpose(1, 0, 2).reshape(rows_s, D_MODEL)
    att_s, hg_s = _proj(xs, n1, w_in_b)
    att_s = att_s.reshape(DEC_SEQ, DEC_BATCH, ATT_SLAB)
    q_r = att_s[:, :, :ATT_WIDTH].reshape(DEC_SEQ, DEC_BATCH, ATT_KV_HEADS, ATT_GROUP, ATT_HEAD_DIM)
    q_r = q_r.transpose(1, 2, 3, 0, 4).reshape(DEC_BATCH, ATT_KV_HEADS, ATT_GROUP * DEC_SEQ, ATT_HEAD_DIM)
    q_zero = jnp.zeros_like(q_r[:, 0])
    q_r = jnp.stack([jnp.concatenate([q_r[:, 0], q_zero], axis=-1),
                     jnp.concatenate([q_zero, q_r[:, 1]], axis=-1)], axis=1)
    kv_new = att_s[:, :, ATT_WIDTH:].transpose(1, 0, 2)
    kv_new = jnp.pad(kv_new, ((0, 0), (0, SAMPLE_KEYS_PAD - SAMPLE_KEYS), (0, 0)))
    o16, new_k_s, new_v_s = _attn_sample(
        q_r, kv_new, cache_k_win[0].reshape(DEC_BATCH, WINDOW, KV_WIDTH),
        cache_v_win[0].reshape(DEC_BATCH, WINDOW, KV_WIDTH), rel_bias, sinks)
    oa_s = o16.reshape(DEC_BATCH, ATT_KV_HEADS, ATT_GROUP, DEC_SEQ, ATT_HEAD_DIM)
    oa_s = oa_s.transpose(3, 0, 1, 2, 4).reshape(rows_s, ATT_WIDTH)
    oh_s, s_s = _hgrn_sample(hg_s.reshape(DEC_SEQ, DEC_BATCH, HG_SLAB), state_hgrn[0], lb_gamma, hgg)
    prev = state_conv[0].transpose(1, 0, 2).reshape((CONV_W - 1) * DEC_BATCH, D_FF)
    y_s, conv_s = _ffn_sample(xs, oa_s, oh_s.reshape(rows_s, HG_WIDTH), *ffn_w, prev)
    y_s = y_s.reshape(DEC_SEQ, DEC_BATCH, D_MODEL).transpose(1, 0, 2)
    conv_s = conv_s.reshape(CONV_W - 1, DEC_BATCH, D_FF).transpose(1, 0, 2)

    cache_shape = (1, DEC_BATCH, WINDOW, ATT_KV_HEADS, ATT_HEAD_DIM)
    return (y_p.reshape(BATCH, SEQ, D_MODEL), y_s, new_k_p, new_v_p, s_p[None], conv_p[None],
            new_k_s.reshape(cache_shape), new_v_s.reshape(cache_shape), s_s[None], conv_s[None])
```

```python
import functools
import math

import jax
import jax.numpy as jnp
import numpy as np
from jax import lax
from jax.experimental import pallas as pl
from jax.experimental.pallas import tpu as pltpu

D_MODEL = 1024
BATCH = 2
SEQ = 8192
DEPTH = 1
DEC_BATCH = 128
DEC_SEQ = 4
ATT_HEAD_DIM = 64
ATT_HEADS = 8
ATT_KV_HEADS = 2
ATT_GROUP = 4
ATT_WIDTH = 512
KV_WIDTH = ATT_KV_HEADS * ATT_HEAD_DIM
WINDOW = 128
NUM_BUCKETS = 32
MAX_DISTANCE = 128
HG_KEY = 128
HG_VAL = 128
HG_HEADS = 4
HG_WIDTH = 512
D_FF = 2816
CONV_W = 3
RMS_EPS = 1e-6
ATT_SLAB = ATT_WIDTH + 2 * KV_WIDTH
HG_SLAB = 4 * HG_WIDTH
PROJ_WIDTH = ATT_SLAB + HG_SLAB

V7X_SUBLANES = 8
V7X_LANES = 128
V7X_MXU_DIM = 256
V7X_VMEM_LIMIT_BYTES = 60 * 1024 * 1024

PROJ_ROWS = 512
LAYER_ROWS = 512
FFN_COLS = V7X_MXU_DIM
HG_CHUNK = 128
SAMPLE_GROUP = 8
SAMPLE_KEYS = WINDOW + DEC_SEQ
SAMPLE_KEYS_PAD = SAMPLE_KEYS + (-SAMPLE_KEYS) % V7X_SUBLANES

_F32 = jnp.float32
_BF16 = jnp.bfloat16
_NEG_INF = float("-inf")
_LOG2_E = 1.0 / math.log(2.0)


def _dot(a, b):
    return jnp.dot(a, b, preferred_element_type=_F32)


def _dot_nt(a, b):
    return lax.dot_general(a, b, (((1,), (1,)), ((), ())), preferred_element_type=_F32)


def _dot_tn(a, b):
    return lax.dot_general(a, b, (((0,), (0,)), ((), ())), preferred_element_type=_F32)


def _rms(x, g):
    return x * lax.rsqrt(jnp.mean(x * x, axis=-1, keepdims=True) + RMS_EPS) * g


def _t5_bucket_np(dist):
    dist = np.asarray(dist)
    n = np.maximum(dist, 0)
    max_exact = NUM_BUCKETS // 2
    nf = np.maximum(n, 1).astype(np.float64)
    large = max_exact + (np.log(nf / max_exact) / math.log(MAX_DISTANCE / max_exact)
                         * (NUM_BUCKETS - max_exact)).astype(np.int32)
    large = np.minimum(large, NUM_BUCKETS - 1)
    bucket = np.where(n < max_exact, n, large)
    return np.where((dist >= 0) & (dist < WINDOW), bucket, -1).astype(np.int32)


def _bias_from_buckets(bidx, relb_ref, head):
    def body(b, acc):
        return jnp.where(bidx == b, relb_ref[b * ATT_HEADS + head], acc)
    return lax.fori_loop(0, NUM_BUCKETS, body, jnp.full(bidx.shape, _NEG_INF, _F32))


def _time_major(ref):
    return jnp.concatenate([ref[:, t, :] for t in range(ref.shape[1])], axis=0)


def _proj_sample_kernel(x_ref, g_ref, w_ref, att_ref, hg_ref):
    h = _rms(_time_major(x_ref), g_ref[...]).astype(_BF16)
    res = _dot(h, w_ref[...])
    att_ref[...] = res[:, :ATT_SLAB]
    hg_ref[...] = res[:, ATT_SLAB:]


def _proj_sample(x, g, w):
    t = x.shape[0] * x.shape[1]

    def full(a):
        return pl.BlockSpec(a.shape, lambda i: (0,) * a.ndim, pipeline_mode=pl.Buffered(1))

    return pl.pallas_call(
        _proj_sample_kernel,
        grid=(1,),
        in_specs=[full(x), full(g), full(w)],
        out_specs=[
            pl.BlockSpec((t, ATT_SLAB), lambda i: (0, 0)),
            pl.BlockSpec((t, HG_SLAB), lambda i: (0, 0)),
        ],
        out_shape=[
            jax.ShapeDtypeStruct((t, ATT_SLAB), _F32),
            jax.ShapeDtypeStruct((t, HG_SLAB), _F32),
        ],
        compiler_params=pltpu.CompilerParams(
            dimension_semantics=("arbitrary",), vmem_limit_bytes=V7X_VMEM_LIMIT_BYTES),
        name="proj_sample",
    )(x, g, w)


def _proj_prompt_kernel(x_ref, g_ref, w_ref, att_ref, hg_ref, tail_ref):
    h = _rms(x_ref[...], g_ref[...]).astype(_BF16)
    res = _dot(h, w_ref[...])
    q = res[:, :ATT_WIDTH] * (ATT_HEAD_DIM ** -0.5)
    att_ref[...] = jnp.concatenate([q, res[:, ATT_WIDTH:ATT_SLAB]], axis=-1).astype(_BF16)
    hg_ref[...] = res[:, ATT_SLAB:]
    tail_ref[...] = res[x_ref.shape[0] - WINDOW:, ATT_WIDTH:ATT_SLAB]


def _proj_prompt(x, g, w, *, rows_per_seq):
    t = x.shape[0]
    tm = PROJ_ROWS
    bps = rows_per_seq // tm
    return pl.pallas_call(
        _proj_prompt_kernel,
        grid=(t // tm,),
        in_specs=[
            pl.BlockSpec((tm, D_MODEL), lambda i: (i, 0)),
            pl.BlockSpec((1, D_MODEL), lambda i: (0, 0)),
            pl.BlockSpec((D_MODEL, PROJ_WIDTH), lambda i: (0, 0)),
        ],
        out_specs=[
            pl.BlockSpec((tm, ATT_SLAB), lambda i: (i, 0)),
            pl.BlockSpec((tm, HG_SLAB), lambda i: (i, 0)),
            pl.BlockSpec((None, WINDOW, 2 * KV_WIDTH), lambda i: (i // bps, 0, 0)),
        ],
        out_shape=[
            jax.ShapeDtypeStruct((t, ATT_SLAB), _BF16),
            jax.ShapeDtypeStruct((t, HG_SLAB), _F32),
            jax.ShapeDtypeStruct((t // rows_per_seq, WINDOW, 2 * KV_WIDTH), _F32),
        ],
        compiler_params=pltpu.CompilerParams(
            dimension_semantics=("arbitrary",), vmem_limit_bytes=V7X_VMEM_LIMIT_BYTES),
        name="proj_prompt",
    )(x, g, w)


def _attn_sample_kernel(relb_ref, sink_ref, bidx_ref, q_ref, kvn_ref, kc_ref, vc_ref,
                        o_ref, nk_ref, nv_ref, bias_scr, sink_scr):
    @pl.when(pl.program_id(0) == 0)
    def _():
        bidx = bidx_ref[...]
        row_g = lax.broadcasted_iota(jnp.int32, (ATT_GROUP * DEC_SEQ, V7X_LANES), 0) // DEC_SEQ
        for kv in range(ATT_KV_HEADS):
            bias = jnp.full(bidx.shape, _NEG_INF, _F32)
            sk = jnp.zeros(row_g.shape, _F32)
            for g in range(ATT_GROUP):
                h = kv * ATT_GROUP + g
                rows = lax.broadcasted_iota(jnp.int32, bidx.shape, 0) // DEC_SEQ == g
                bias = jnp.where(rows, _bias_from_buckets(bidx, relb_ref, h), bias)
                sk = jnp.where(row_g == g, sink_ref[h], sk)
            bias_scr[kv] = bias
            sink_scr[kv] = sk

    keep = WINDOW - DEC_SEQ
    units = [(b, kv) for b in range(SAMPLE_GROUP) for kv in range(ATT_KV_HEADS)]
    scores = {}
    for b in range(SAMPLE_GROUP):
        kvn = kvn_ref[b]
        nk_ref[b, :keep, :] = kc_ref[b, DEC_SEQ:, :]
        nv_ref[b, :keep, :] = vc_ref[b, DEC_SEQ:, :]
        nk_ref[b, keep:, :] = kvn[:DEC_SEQ, :KV_WIDTH]
        nv_ref[b, keep:, :] = kvn[:DEC_SEQ, KV_WIDTH:]
        k = jnp.concatenate([kc_ref[b], kvn[:, :KV_WIDTH]], axis=0).astype(_BF16)
        for kv in range(ATT_KV_HEADS):
            q = (q_ref[b, kv] * (ATT_HEAD_DIM ** -0.5)).astype(_BF16)
            scores[b, kv] = _dot_nt(q, k)
    probs = {}
    for b, kv in units:
        s = scores[b, kv] + bias_scr[kv]
        sink = sink_scr[kv][:, :1]
        m = jnp.maximum(jnp.max(s, axis=-1, keepdims=True), sink)
        p = jnp.exp(s - m)
        den = jnp.sum(p, axis=-1, keepdims=True) + jnp.exp(sink - m)
        probs[b, kv] = (p.astype(_BF16), den)
    for b in range(SAMPLE_GROUP):
        v = jnp.concatenate([vc_ref[b], kvn_ref[b][:, KV_WIDTH:]], axis=0).astype(_BF16)
        for kv in range(ATT_KV_HEADS):
            p, den = probs[b, kv]
            o = _dot(p, v) / den
            o_ref[b, kv] = o[:, kv * ATT_HEAD_DIM:(kv + 1) * ATT_HEAD_DIM]


def _attn_sample(q_r, kv_new, cache_k, cache_v, rel_bias, sinks):
    db = q_r.shape[0]
    rows = ATT_GROUP * DEC_SEQ
    new_rows = SAMPLE_KEYS_PAD - WINDOW
    t = (np.arange(rows) % DEC_SEQ)[:, None]
    c = np.arange(SAMPLE_KEYS_PAD)[None, :]
    dist = np.where(c < SAMPLE_KEYS, t + WINDOW - c, -1)
    bidx = jnp.asarray(_t5_bucket_np(dist))
    smem = pl.BlockSpec(memory_space=pltpu.SMEM)
    g = SAMPLE_GROUP
    return pl.pallas_call(
        _attn_sample_kernel,
        grid=(db // g,),
        in_specs=[
            smem, smem,
            pl.BlockSpec((rows, SAMPLE_KEYS_PAD), lambda i: (0, 0)),
            pl.BlockSpec((g, ATT_KV_HEADS, rows, KV_WIDTH), lambda i: (i, 0, 0, 0)),
            pl.BlockSpec((g, new_rows, 2 * KV_WIDTH), lambda i: (i, 0, 0)),
            pl.BlockSpec((g, WINDOW, KV_WIDTH), lambda i: (i, 0, 0)),
            pl.BlockSpec((g, WINDOW, KV_WIDTH), lambda i: (i, 0, 0)),
        ],
        out_specs=[
            pl.BlockSpec((g, ATT_KV_HEADS, rows, ATT_HEAD_DIM), lambda i: (i, 0, 0, 0)),
            pl.BlockSpec((g, WINDOW, KV_WIDTH), lambda i: (i, 0, 0)),
            pl.BlockSpec((g, WINDOW, KV_WIDTH), lambda i: (i, 0, 0)),
        ],
        out_shape=[
            jax.ShapeDtypeStruct((db, ATT_KV_HEADS, rows, ATT_HEAD_DIM), _F32),
            jax.ShapeDtypeStruct((db, WINDOW, KV_WIDTH), _F32),
            jax.ShapeDtypeStruct((db, WINDOW, KV_WIDTH), _F32),
        ],
        scratch_shapes=[
            pltpu.VMEM((ATT_KV_HEADS, rows, SAMPLE_KEYS_PAD), _F32),
            pltpu.VMEM((ATT_KV_HEADS, rows, V7X_LANES), _F32),
        ],
        compiler_params=pltpu.CompilerParams(dimension_semantics=("arbitrary",)),
        name="attn_sample",
    )(rel_bias.reshape(-1), sinks, bidx, q_r, kv_new, cache_k, cache_v)


def _lower_bound(lbg):
    mx = jnp.max(lbg, axis=0, keepdims=True)
    e = jnp.exp(lbg - mx)
    return e[0:1] / jnp.sum(e, axis=0, keepdims=True)


def _hgrn_gates(hg, lb):
    q = jax.nn.silu(hg[:, 0:HG_WIDTH])
    f = lb + (1.0 - lb) * jax.nn.sigmoid(hg[:, HG_WIDTH:2 * HG_WIDTH])
    v = hg[:, 2 * HG_WIDTH:3 * HG_WIDTH]
    gate = jax.nn.silu(hg[:, 3 * HG_WIDTH:4 * HG_WIDTH])
    return q, jnp.log(f), 1.0 - f, v, gate


def _hgrn_levels(c):
    return [c >> (i + 1) for i in range(int(math.log2(c)))]


def _hgrn_small_levels(c):
    return [h for h in _hgrn_levels(c) if 2 * h < V7X_SUBLANES]


def _hgrn_tables(c):
    lvl = np.full((c, c), -1, np.int32)
    j = np.arange(c)[None, :]
    t = np.arange(c)[:, None]
    mats = [j <= t]
    for li, h in enumerate(_hgrn_levels(c)):
        mid = (t // (2 * h)) * (2 * h) + h
        same = (t // (2 * h)) == (j // (2 * h))
        lvl[same & (t >= mid) & (j < mid)] = li
        if h in _hgrn_small_levels(c):
            mats.append(np.where(t >= mid, (j >= mid) & (j <= t), (j > t) & (j < mid)))
    return np.concatenate(mats, axis=0).astype(np.float32), np.concatenate([lvl, lvl], axis=1)


def _pair_blockdiag(x):
    left = lax.broadcasted_iota(jnp.int32, x.shape, 1) < HG_KEY
    zero = jnp.zeros_like(x)
    return jnp.concatenate([jnp.where(left, x, zero), jnp.where(left, zero, x)], axis=0)


def _exact_rows_matmul_stacked(m3, x):
    h1 = x.astype(_BF16)
    r1 = x - h1.astype(_F32)
    h2 = r1.astype(_BF16)
    h3 = (r1 - h2.astype(_F32)).astype(_BF16)
    return _dot(m3, jnp.concatenate([h3, h2, h1], axis=0))


def _hgrn_level_rows(h, c):
    return [(slice(j * 2 * h, j * 2 * h + h), slice(j * 2 * h + h, (j + 1) * 2 * h))
            for j in range(c // (2 * h))]


def _hgrn_level_operands(q, k, sums, cb_ref, li, c):
    h = _hgrn_levels(c)[li]
    cb = sums[:c]
    small = _hgrn_small_levels(c)
    if h % V7X_SUBLANES == 0:
        lhs, rhs = [], []
        for j, (up, low) in enumerate(_hgrn_level_rows(h, c)):
            mid = cb_ref[pl.ds(j * 2 * h + h - 1, 1), :]
            lhs.append(q[low] * jnp.exp2(cb[low] - mid))
            rhs.append(k[up] * jnp.exp2(mid - cb[up]))
            rhs.append(jnp.zeros_like(k[low]))
        return (jnp.concatenate(lhs, axis=0).astype(_BF16),
                jnp.concatenate(rhs, axis=0).astype(_BF16), True)
    if h in small:
        i = 1 + small.index(h)
        arg = sums[i * c:(i + 1) * c]
    else:
        mid_cb = jnp.concatenate(
            [jnp.broadcast_to(cb_ref[pl.ds(j * 2 * h + h - 1, 1), :], (2 * h, cb.shape[1]))
             for j in range(c // (2 * h))], axis=0)
        arg = -jnp.abs(cb - mid_cb)
    e = jnp.exp2(arg)
    return (q * e).astype(_BF16), (k * e).astype(_BF16), False


def _hgrn_merge_level(a, pr, lvl, li, lower_only, c):
    if not lower_only:
        return jnp.where(lvl == li, pr, a)
    h = _hgrn_levels(c)[li]
    pieces = []
    for j, (up, low) in enumerate(_hgrn_level_rows(h, c)):
        pieces.append(a[up])
        pieces.append(jnp.where(lvl[low] == li, pr[j * h:(j + 1) * h], a[low]))
    return jnp.concatenate(pieces, axis=0)


def _hgrn_sample_kernel(hg_ref, lbg_ref, g_ref, mask_ref, s0_ref, o_ref, s_ref):
    nb = SAMPLE_GROUP
    lb = _lower_bound(lbg_ref[...])
    q, cb, k, v, gate = [], [], [], [], []
    for t in range(DEC_SEQ):
        qt, lf, kt, vt, gt = _hgrn_gates(hg_ref[t], lb)
        q.append(qt); k.append(kt); v.append(vt); gate.append(gt)
        cb.append(lf if t == 0 else cb[-1] + lf)
    total = cb[-1]
    mask = mask_ref[...]
    for h in range(HG_HEADS):
        hs = slice(h * HG_KEY, (h + 1) * HG_KEY)
        qdec = jnp.concatenate([q[t][:, hs] * jnp.exp(cb[t][:, hs]) for t in range(DEC_SEQ)], axis=0)
        kend = jnp.concatenate(
            [k[t][:, hs] * jnp.exp(total[:, hs] - cb[t][:, hs]) for t in range(DEC_SEQ)], axis=0)
        vall = jnp.concatenate([v[t][:, hs] for t in range(DEC_SEQ)], axis=0).astype(_BF16)
        qblk = (jnp.concatenate([qdec] * nb, axis=1) * mask).astype(_BF16)
        kblk = (jnp.concatenate([kend] * nb, axis=1) * mask).astype(_BF16)
        s0 = s0_ref[:, h]
        o_inter = _dot(qblk, s0.reshape(nb * HG_KEY, HG_VAL).astype(_BF16))
        upd = _dot_tn(kblk, vall)
        decay_t = jnp.exp(total[:, hs]).T
        for b in range(nb):
            s_ref[b, h] = s0[b] * decay_t[:, b:b + 1] + upd[b * HG_KEY:(b + 1) * HG_KEY]
        for t in range(DEC_SEQ):
            o = o_inter[t * nb:(t + 1) * nb]
            for s in range(t + 1):
                w = q[t][:, hs] * k[s][:, hs]
                if s < t:
                    w = w * jnp.exp(cb[t][:, hs] - cb[s][:, hs])
                o = o + jnp.sum(w, axis=-1, keepdims=True) * v[s][:, hs]
            o = o * lax.rsqrt(jnp.mean(o * o, axis=-1, keepdims=True) + RMS_EPS)
            o_ref[t, :, hs] = o * g_ref[:, hs] * gate[t][:, hs]


def _hgrn_sample(hg, s0, lb_gamma, hg_g):
    l, db, _ = hg.shape
    nb = SAMPLE_GROUP
    rb = np.arange(l * nb)[:, None] % nb
    cbk = np.arange(nb * HG_KEY)[None, :] // HG_KEY
    mask = jnp.asarray((rb == cbk).astype(np.float32))
    return pl.pallas_call(
        _hgrn_sample_kernel,
        grid=(db // nb,),
        in_specs=[
            pl.BlockSpec((l, nb, HG_SLAB), lambda i: (0, i, 0)),
            pl.BlockSpec(lb_gamma.shape, lambda i: (0, 0)),
            pl.BlockSpec((1, HG_WIDTH), lambda i: (0, 0)),
            pl.BlockSpec(mask.shape, lambda i: (0, 0)),
            pl.BlockSpec((nb, HG_HEADS, HG_KEY, HG_VAL), lambda i: (i, 0, 0, 0)),
        ],
        out_specs=[
            pl.BlockSpec((l, nb, HG_WIDTH), lambda i: (0, i, 0)),
            pl.BlockSpec((nb, HG_HEADS, HG_KEY, HG_VAL), lambda i: (i, 0, 0, 0)),
        ],
        out_shape=[
            jax.ShapeDtypeStruct((l, db, HG_WIDTH), _F32),
            jax.ShapeDtypeStruct((db, HG_HEADS, HG_KEY, HG_VAL), _F32),
        ],
        compiler_params=pltpu.CompilerParams(
            dimension_semantics=("arbitrary",), vmem_limit_bytes=V7X_VMEM_LIMIT_BYTES),
        name="hgrn_sample",
    )(hg, lb_gamma, hg_g, mask, s0)


def _col_chunks(n, wide):
    chunks = [(lo, wide) for lo in range(0, n - n % wide, wide)]
    if n % wide:
        chunks.append((n - n % wide, n % wide))
    return chunks


def _ffn_col_chunks(n):
    return _col_chunks(n, FFN_COLS)


def _ffn_sample_kernel(x_ref, oa_ref, oh_ref, ag_ref, wo_ref, n2_ref, wfi_ref, cw_ref, cb_ref, wfo_ref,
                       fg_ref, prev_ref, y_ref, conv_ref, buf, hid):
    shift = x_ref.shape[0]
    tm = shift * x_ref.shape[1]
    pad = buf.shape[0] - tm
    oa = _rms(oa_ref[...], ag_ref[...])
    mix = jnp.concatenate([oa.astype(_BF16), oh_ref[...].astype(_BF16)], axis=-1)
    x1 = _time_major(x_ref) + _dot(mix, wo_ref[...])
    h2 = _rms(x1, n2_ref[...]).astype(_BF16)
    buf[:pad, :] = _time_major(prev_ref)
    for lo, width in _ffn_col_chunks(D_FF):
        cs = slice(lo, lo + width)
        a = _dot(h2, wfi_ref[:, cs])
        gate = _dot(h2, wfi_ref[:, D_FF + lo:D_FF + lo + width])
        buf[pad:, cs] = a
        ac = cb_ref[:, cs] + buf[pad - 2 * shift:pad - 2 * shift + tm, cs] * cw_ref[0:1, cs]
        ac = ac + buf[pad - shift:pad - shift + tm, cs] * cw_ref[1:2, cs]
        ac = ac + a * cw_ref[2:3, cs]
        hid[:, cs] = (jax.nn.silu(ac) * gate).astype(_BF16)
    y = _rms(x1 + _dot(hid[...], wfo_ref[...]), fg_ref[...])
    for t in range(y_ref.shape[1]):
        y_ref[:, t, :] = y[t * shift:(t + 1) * shift]
    for j in range(conv_ref.shape[1]):
        conv_ref[:, j, :] = buf[tm + j * shift:tm + (j + 1) * shift, :]


def _ffn_sample(x, oa, oh, attn_g, w_o, n2, wf_in, conv_w, conv_b, wf_out, final_g, prev):
    t = x.shape[0] * x.shape[1]
    pad = prev.shape[0] * prev.shape[1]

    def full(a):
        return pl.BlockSpec(a.shape, lambda i: (0,) * a.ndim, pipeline_mode=pl.Buffered(1))

    args = (x, oa, oh, attn_g, w_o, n2, wf_in, conv_w, conv_b, wf_out, final_g, prev)
    return pl.pallas_call(
        _ffn_sample_kernel,
        grid=(1,),
        in_specs=[full(a) for a in args],
        out_specs=[pl.BlockSpec(x.shape, lambda i: (0, 0, 0)),
                   pl.BlockSpec(prev.shape, lambda i: (0, 0, 0))],
        out_shape=[jax.ShapeDtypeStruct(x.shape, _F32),
                   jax.ShapeDtypeStruct(prev.shape, _F32)],
        scratch_shapes=[pltpu.VMEM((pad + t, D_FF), _F32), pltpu.VMEM((t, D_FF), _BF16)],
        compiler_params=pltpu.CompilerParams(
            dimension_semantics=("arbitrary",), vmem_limit_bytes=V7X_VMEM_LIMIT_BYTES),
        name="ffn_sample",
    )(*args)


def _interleave(streams):
    done = [0] * len(streams)
    live = set(range(len(streams)))
    while live:
        idx = min(live, key=lambda s: (done[s] + 1) / streams[s][1])
        try:
            next(streams[idx][0])
            done[idx] += 1
        except StopIteration:
            live.discard(idx)


def _attn_stream(q_ref, kvp_ref, kvc_ref, bias_scr, ag_ref, out_ref, seq_start):
    w = WINDOW
    pair = 2 * ATT_HEAD_DIM
    half_g = ATT_GROUP // 2
    rows = q_ref.shape[0]
    kvx = jnp.concatenate([kvp_ref[...], kvc_ref[...]], axis=0)
    kx = kvx[:, :pair]
    kr = pltpu.roll(kx.astype(_F32), ATT_HEAD_DIM, 1).astype(_BF16)
    lo = lax.broadcasted_iota(jnp.int32, kx.shape, 1) < ATT_HEAD_DIM
    zero = jnp.zeros_like(kx)
    kmat = ((jnp.where(lo, kx, zero), jnp.where(lo, kr, zero)),
            (jnp.where(lo, zero, kr), jnp.where(lo, zero, kx)))
    v_t = kvx[:, pair:].astype(_F32).T.astype(_BF16)
    row = lax.broadcasted_iota(jnp.int32, (2 * w, half_g * w), 0)
    first_key = lax.broadcasted_iota(jnp.int32, (2 * w, pair), 0) == 0
    first_col = lax.broadcasted_iota(jnp.int32, (ATT_HEAD_DIM, 2 * w), 1) == 0
    no_prev = (row >= 1) & (row < w) & seq_start
    units = [(kv, par) for kv in range(ATT_KV_HEADS) for par in range(2)]
    yield
    for j in range(rows // w):
        qs = slice(j * w, (j + 1) * w)
        ks = slice(j * w, (j + 2) * w)
        scores = {}
        for kv in range(ATT_KV_HEADS):
            base = kv * ATT_GROUP * ATT_HEAD_DIM
            lhs = jnp.concatenate(
                [q_ref[qs, base + i * pair:base + (i + 1) * pair] for i in range(half_g)], axis=0)
            for par in range(2):
                keys = jnp.where(first_key, jnp.zeros((), _BF16), kmat[par][kv][ks])
                scores[kv, par] = _dot_nt(keys, lhs)
        yield
        probs = {}
        for kv, par in units:
            s = scores[kv, par] + bias_scr[kv, par]
            if j == 0:
                s = jnp.where(no_prev, _NEG_INF, s)
            m = jnp.max(s, axis=0, keepdims=True)
            p = jnp.exp(s - m)
            probs[kv, par] = (p.astype(_BF16), jnp.sum(p, axis=0, keepdims=True))
        yield
        outs = {}
        for kv, par in units:
            p, den = probs[kv, par]
            vals = v_t[kv * ATT_HEAD_DIM:(kv + 1) * ATT_HEAD_DIM, ks]
            vals = jnp.where(first_col, jnp.zeros((), _BF16), vals)
            outs[kv, par] = _dot(vals, p) / den
        pieces = []
        for kv in range(ATT_KV_HEADS):
            for i in range(half_g):
                both = jnp.concatenate(
                    [outs[kv, par][:, i * w:(i + 1) * w] for par in range(2)], axis=0)
                pieces.append(both.T)
        o = jnp.concatenate(pieces, axis=-1)
        out_ref[qs, :ATT_WIDTH] = _rms(o, ag_ref[...]).astype(_BF16)
        yield


def _hgrn_stream(hg_ref, lbg_ref, g_ref, m_ref, lvl_ref, st_scr, cb_scr, out_ref, seq_start):
    c = HG_CHUNK
    pw = 2 * HG_KEY
    pairs = range(HG_HEADS // 2)
    lb = _lower_bound(lbg_ref[...])
    lvl = lvl_ref[...]
    rr = lax.broadcasted_iota(jnp.int32, (pw, pw), 0) < HG_VAL
    cc = lax.broadcasted_iota(jnp.int32, (pw, pw), 1) < HG_KEY
    same_head = rr == cc
    for ci in range(hg_ref.shape[0] // c):
        rs = slice(ci * c, (ci + 1) * c)
        q, k, v, gate, sums = {}, {}, {}, {}, {}
        for p in pairs:
            ps = slice(p * pw, (p + 1) * pw)
            q[p] = jax.nn.silu(hg_ref[rs, p * pw:(p + 1) * pw])
            f = lb[:, ps] + (1.0 - lb[:, ps]) * jax.nn.sigmoid(
                hg_ref[rs, HG_WIDTH + p * pw:HG_WIDTH + (p + 1) * pw])
            v[p] = hg_ref[rs, 2 * HG_WIDTH + p * pw:2 * HG_WIDTH + (p + 1) * pw]
            gate[p] = jax.nn.silu(hg_ref[rs, 3 * HG_WIDTH + p * pw:3 * HG_WIDTH + (p + 1) * pw])
            k[p] = 1.0 - f
            sums[p] = _exact_rows_matmul_stacked(m_ref[...], jnp.log(f) * _LOG2_E)
        yield
        for p in pairs:
            cb_scr[p] = sums[p][:c]
        levels = _hgrn_levels(c)
        a = {p: jnp.zeros((c, pw), _F32) for p in pairs}
        ops = {p: _hgrn_level_operands(q[p], k[p], sums[p], cb_scr.at[p], 0, c) for p in pairs}
        yield
        for li in range(len(levels)):
            nxt = None
            if li + 1 < len(levels):
                nxt = {p: _hgrn_level_operands(q[p], k[p], sums[p], cb_scr.at[p], li + 1, c)
                       for p in pairs}
            for p in pairs:
                lhs, rhs, low = ops[p]
                pr = _dot_nt(lhs, _pair_blockdiag(rhs))
                a[p] = _hgrn_merge_level(a[p], pr, lvl, li, low, c)
            ops = nxt
            yield
        for p in pairs:
            cb = sums[p][:c]
            total = cb[c - 1:c, :]
            vb = v[p].astype(_BF16)
            st = st_scr[p]
            if ci == 0:
                st = jnp.where(seq_start, 0.0, st)
            o = (_dot_nt((q[p] * jnp.exp2(cb)).astype(_BF16), st.astype(_BF16))
                 + _dot(a[p].astype(_BF16), _pair_blockdiag(vb)))
            upd = _dot_tn(vb, (k[p] * jnp.exp2(total - cb)).astype(_BF16))
            st_scr[p] = st * jnp.exp2(total) + jnp.where(same_head, upd, 0.0)
            qk = q[p] * k[p]
            for hh in range(2):
                hs = slice(hh * HG_KEY, (hh + 1) * HG_KEY)
                gs = slice(p * pw + hh * HG_KEY, p * pw + (hh + 1) * HG_KEY)
                oh = o[:, hs] + jnp.sum(qk[:, hs], axis=-1, keepdims=True) * v[p][:, hs]
                oh = oh * lax.rsqrt(jnp.mean(oh * oh, axis=-1, keepdims=True) + RMS_EPS)
                out_ref[rs, ATT_WIDTH + gs.start:ATT_WIDTH + gs.stop] = (
                    oh * g_ref[:, gs] * gate[p][:, hs]).astype(_BF16)
        yield


def _ffn_stream(mix_ref, x_ref, wo_ref, n2_ref, wfi_ref, cw_ref, cb_ref, wfo_ref, fg_ref,
                y_ref, conv_ref, cbuf, carry, hid, seq_start):
    tm = x_ref.shape[0]
    pad = carry.shape[0]
    x1 = x_ref[...] + _dot(mix_ref[...], wo_ref[...])
    y_ref[...] = x1
    h2 = _rms(x1, n2_ref[...]).astype(_BF16)
    yield
    for c, (lo, width) in enumerate(_ffn_col_chunks(D_FF)):
        cs = slice(lo, lo + width)
        buf = cbuf.at[c % 2]
        a = _dot(h2, wfi_ref[:, cs])
        gate = _dot(h2, wfi_ref[:, D_FF + lo:D_FF + lo + width])
        buf[:pad, :width] = jnp.where(seq_start, 0.0, carry[:, cs])
        buf[pad:, :width] = a
        carry[:, cs] = a[tm - pad:, :]
        ac = cb_ref[:, cs] + buf[pad - 2:pad - 2 + tm, :width] * cw_ref[0:1, cs]
        ac = ac + buf[pad - 1:pad - 1 + tm, :width] * cw_ref[1:2, cs]
        ac = ac + a * cw_ref[2:3, cs]
        hid[:, cs] = (jax.nn.silu(ac) * gate).astype(_BF16)
        yield
    conv_ref[...] = carry[pad - (CONV_W - 1):, :]
    for lo, width in _ffn_col_chunks(D_MODEL):
        ns = slice(lo, lo + width)
        y_ref[:, ns] = y_ref[:, ns] + _dot(hid[...], wfo_ref[:, ns])
        yield
    y_ref[...] = _rms(y_ref[...], fg_ref[...])


def _layer_kernel(relb_ref, sink_ref, bidx_ref, q_ref, kvp_ref, kvc_ref, hg_ref, lbg_ref, hgg_ref,
                  m_ref, lvl_ref, x_ref, ag_ref, wo_ref, n2_ref, wfi_ref, cw_ref, cb_ref, wfo_ref,
                  fg_ref, y_ref, conv_ref, s_ref, bias_scr, st_scr, cb_scr, mix_scr, cbuf, carry, hid,
                  *, blocks_per_seq):
    i = pl.program_id(0)
    nblk = pl.num_programs(0) - 1
    w = WINDOW

    @pl.when(i == 0)
    def _():
        bidx = bidx_ref[...]
        key0 = lax.broadcasted_iota(jnp.int32, bidx.shape, 0) == 0
        for kv in range(ATT_KV_HEADS):
            for par in range(2):
                for g in range(ATT_GROUP // 2):
                    h = kv * ATT_GROUP + 2 * g + par
                    tab = _bias_from_buckets(bidx, relb_ref, h)
                    bias_scr[kv, par, :, g * w:(g + 1) * w] = jnp.where(key0, sink_ref[h], tab)
        st_scr[...] = jnp.zeros(st_scr.shape, _F32)
        mix_scr[...] = jnp.zeros(mix_scr.shape, _BF16)
        carry[...] = jnp.zeros(carry.shape, _F32)

    slot = i % 2
    mix_start = (jnp.minimum(i, nblk - 1) % blocks_per_seq) == 0
    ffn_start = (jnp.maximum(i - 1, 0) % blocks_per_seq) == 0
    rows = q_ref.shape[0]
    chunks = rows // HG_CHUNK
    _interleave([
        (_ffn_stream(mix_scr.at[1 - slot], x_ref, wo_ref, n2_ref, wfi_ref, cw_ref, cb_ref, wfo_ref,
                     fg_ref, y_ref, conv_ref, cbuf, carry, hid, ffn_start),
         1 + len(_ffn_col_chunks(D_FF)) + len(_ffn_col_chunks(D_MODEL)) + 1),
        (_hgrn_stream(hg_ref, lbg_ref, hgg_ref, m_ref, lvl_ref, st_scr, cb_scr, mix_scr.at[slot],
                      mix_start), chunks * (3 + len(_hgrn_levels(HG_CHUNK)))),
        (_attn_stream(q_ref, kvp_ref, kvc_ref, bias_scr, ag_ref, mix_scr.at[slot], mix_start),
         1 + 3 * (rows // w)),
    ])

    @pl.when((i % blocks_per_seq == blocks_per_seq - 1) & (i < nblk))
    def _():
        for p in range(HG_HEADS // 2):
            st = st_scr[p]
            s_ref[2 * p] = st[:HG_VAL, :HG_KEY].T
            s_ref[2 * p + 1] = st[HG_VAL:, HG_KEY:].T


def _layer(x, att, hg, rel_bias, sinks, lb_gamma, hg_g, attn_g, w_o, n2, wf_in, conv_w, conv_b,
           wf_out, final_g, *, rows_per_seq):
    t = x.shape[0]
    rows = LAYER_ROWS
    nblk = t // rows
    bps = rows_per_seq // rows
    nseq = nblk // bps
    pad = V7X_SUBLANES
    qi = np.arange(WINDOW)[None, :]
    si = np.arange(2 * WINDOW)[:, None]
    bidx = jnp.asarray(_t5_bucket_np(WINDOW + qi - si))
    mats, lvl = _hgrn_tables(HG_CHUNK)
    mats = np.tile(mats, (1, 3))
    kvcol = ATT_WIDTH // (2 * KV_WIDTH)
    wblocks = rows // WINDOW
    last = nblk - 1
    smem = pl.BlockSpec(memory_space=pltpu.SMEM)
    const = functools.partial(pl.BlockSpec, pipeline_mode=pl.Buffered(1))

    def mix_blk(i):
        return jnp.minimum(i, last)

    def ffn_blk(i):
        return jnp.maximum(i - 1, 0)

    in_specs = [
        smem, smem,
        const(bidx.shape, lambda i: (0, 0)),
        pl.BlockSpec((rows, ATT_WIDTH), lambda i: (mix_blk(i), 0)),
        pl.BlockSpec((WINDOW, 2 * KV_WIDTH),
                     lambda i: (jnp.maximum(mix_blk(i) * wblocks - 1, 0), kvcol)),
        pl.BlockSpec((rows, 2 * KV_WIDTH), lambda i: (mix_blk(i), kvcol)),
        pl.BlockSpec((rows, HG_SLAB), lambda i: (mix_blk(i), 0)),
        const(lb_gamma.shape, lambda i: (0, 0)),
        const((1, HG_WIDTH), lambda i: (0, 0)),
        const(mats.shape, lambda i: (0, 0)),
        const(lvl.shape, lambda i: (0, 0)),
        pl.BlockSpec((rows, D_MODEL), lambda i: (ffn_blk(i), 0)),
        const((1, ATT_WIDTH), lambda i: (0, 0)),
        const((ATT_WIDTH + HG_WIDTH, D_MODEL), lambda i: (0, 0)),
        const((1, D_MODEL), lambda i: (0, 0)),
        const((D_MODEL, 2 * D_FF), lambda i: (0, 0)),
        const((CONV_W, D_FF), lambda i: (0, 0)),
        const((1, D_FF), lambda i: (0, 0)),
        const((D_FF, D_MODEL), lambda i: (0, 0)),
        const((1, D_MODEL), lambda i: (0, 0)),
    ]
    return pl.pallas_call(
        functools.partial(_layer_kernel, blocks_per_seq=bps),
        grid=(nblk + 1,),
        in_specs=in_specs,
        out_specs=[
            pl.BlockSpec((rows, D_MODEL), lambda i: (ffn_blk(i), 0)),
            pl.BlockSpec((None, CONV_W - 1, D_FF), lambda i: (ffn_blk(i) // bps, 0, 0)),
            pl.BlockSpec((None, HG_HEADS, HG_KEY, HG_VAL), lambda i: (mix_blk(i) // bps, 0, 0, 0)),
        ],
        out_shape=[
            jax.ShapeDtypeStruct((t, D_MODEL), _F32),
            jax.ShapeDtypeStruct((nseq, CONV_W - 1, D_FF), _F32),
            jax.ShapeDtypeStruct((nseq, HG_HEADS, HG_KEY, HG_VAL), _F32),
        ],
        scratch_shapes=[
            pltpu.VMEM((ATT_KV_HEADS, 2, 2 * WINDOW, ATT_GROUP // 2 * WINDOW), _F32),
            pltpu.VMEM((HG_HEADS // 2, 2 * HG_VAL, 2 * HG_KEY), _F32),
            pltpu.VMEM((HG_HEADS // 2, HG_CHUNK, 2 * HG_KEY), _F32),
            pltpu.VMEM((2, rows, ATT_WIDTH + HG_WIDTH), _BF16),
            pltpu.VMEM((2, pad + rows, FFN_COLS), _F32),
            pltpu.VMEM((pad, D_FF), _F32),
            pltpu.VMEM((rows, D_FF), _BF16),
        ],
        compiler_params=pltpu.CompilerParams(
            dimension_semantics=("arbitrary",), vmem_limit_bytes=V7X_VMEM_LIMIT_BYTES),
        name="layer",
    )(rel_bias.reshape(-1), sinks, bidx, att, att, att, hg, lb_gamma, hg_g, jnp.asarray(mats, _BF16),
      jnp.asarray(lvl), x, attn_g, w_o, n2, wf_in, conv_w, conv_b, wf_out, final_g)


def kernel(x_prompt, x_sample, cache_k_win, cache_v_win, state_hgrn, state_conv, norm1_g, w_in,
           attn_sinks, rel_bias, lb_gamma, attn_out_g, hg_out_g, w_out, norm2_g, w_ffn_in, conv_w,
           conv_b, w_ffn_out, final_g):
    assert DEPTH == 1
    n1 = norm1_g[0][None]
    n2 = norm2_g[0][None]
    ag = attn_out_g[0][None]
    hgg = hg_out_g[0][None]
    fg = final_g[None]
    cb = conv_b[0][None]
    cw = conv_w[0]
    sinks = attn_sinks[0]
    w_in_b = w_in[0].astype(_BF16)
    w_o_b = w_out[0].astype(_BF16)
    wf_in_b = w_ffn_in[0].astype(_BF16)
    wf_out_b = w_ffn_out[0].astype(_BF16)
    ffn_w = (ag, w_o_b, n2, wf_in_b, cw, cb, wf_out_b, fg)

    xp = x_prompt.reshape(BATCH * SEQ, D_MODEL)
    att_p, hg_p, kv_tail = _proj_prompt(xp, n1, w_in_b, rows_per_seq=SEQ)
    y_p, conv_p, s_p = _layer(xp, att_p, hg_p, rel_bias, sinks, lb_gamma, hgg, *ffn_w,
                              rows_per_seq=SEQ)
    new_k_p = kv_tail[:, :, :KV_WIDTH].reshape(1, BATCH, WINDOW, ATT_KV_HEADS, ATT_HEAD_DIM)
    new_v_p = kv_tail[:, :, KV_WIDTH:].reshape(1, BATCH, WINDOW, ATT_KV_HEADS, ATT_HEAD_DIM)

    rows_s = DEC_SEQ * DEC_BATCH
    att_s, hg_s = _proj_sample(x_sample, n1, w_in_b)
    att_s = att_s.reshape(DEC_SEQ, DEC_BATCH, ATT_SLAB)
    q_r = att_s[:, :, :ATT_WIDTH].reshape(DEC_SEQ, DEC_BATCH, ATT_KV_HEADS, ATT_GROUP, ATT_HEAD_DIM)
    q_r = q_r.transpose(1, 2, 3, 0, 4).reshape(DEC_BATCH, ATT_KV_HEADS, ATT_GROUP * DEC_SEQ, ATT_HEAD_DIM)
    q_zero = jnp.zeros_like(q_r[:, 0])
    q_r = jnp.stack([jnp.concatenate([q_r[:, 0], q_zero], axis=-1),
                     jnp.concatenate([q_zero, q_r[:, 1]], axis=-1)], axis=1)
    kv_new = att_s[:, :, ATT_WIDTH:].transpose(1, 0, 2)
    kv_new = jnp.pad(kv_new, ((0, 0), (0, SAMPLE_KEYS_PAD - SAMPLE_KEYS), (0, 0)))
    o16, new_k_s, new_v_s = _attn_sample(
        q_r, kv_new, cache_k_win[0].reshape(DEC_BATCH, WINDOW, KV_WIDTH),
        cache_v_win[0].reshape(DEC_BATCH, WINDOW, KV_WIDTH), rel_bias, sinks)
    oa_s = o16.reshape(DEC_BATCH, ATT_KV_HEADS, ATT_GROUP, DEC_SEQ, ATT_HEAD_DIM)
    oa_s = oa_s.transpose(3, 0, 1, 2, 4).reshape(rows_s, ATT_WIDTH)
    oh_s, s_s = _hgrn_sample(hg_s.reshape(DEC_SEQ, DEC_BATCH, HG_SLAB), state_hgrn[0], lb_gamma, hgg)
    y_s, conv_s = _ffn_sample(x_sample, oa_s, oh_s.reshape(rows_s, HG_WIDTH), *ffn_w, state_conv[0])

    cache_shape = (1, DEC_BATCH, WINDOW, ATT_KV_HEADS, ATT_HEAD_DIM)
    return (y_p.reshape(BATCH, SEQ, D_MODEL), y_s, new_k_p, new_v_p, s_p[None], conv_p[None],
            new_k_s.reshape(cache_shape), new_v_s.reshape(cache_shape), s_s[None], conv_s[None])
```

```python
import functools
import math

import jax
import jax.numpy as jnp
import numpy as np
from jax import lax
from jax.experimental import pallas as pl
from jax.experimental.pallas import tpu as pltpu

D_MODEL = 1024
BATCH = 2
SEQ = 8192
DEPTH = 1
DEC_BATCH = 128
DEC_SEQ = 4
ATT_HEAD_DIM = 64
ATT_HEADS = 8
ATT_KV_HEADS = 2
ATT_GROUP = 4
ATT_WIDTH = 512
KV_WIDTH = ATT_KV_HEADS * ATT_HEAD_DIM
WINDOW = 128
NUM_BUCKETS = 32
MAX_DISTANCE = 128
HG_KEY = 128
HG_VAL = 128
HG_HEADS = 4
HG_WIDTH = 512
D_FF = 2816
CONV_W = 3
RMS_EPS = 1e-6
ATT_SLAB = ATT_WIDTH + 2 * KV_WIDTH
HG_SLAB = 4 * HG_WIDTH
PROJ_WIDTH = ATT_SLAB + HG_SLAB

V7X_SUBLANES = 8
V7X_LANES = 128
V7X_MXU_DIM = 256
V7X_VMEM_LIMIT_BYTES = 60 * 1024 * 1024

PROJ_ROWS = 512
WEIGHT_CAST_STEPS = 16
LAYER_ROWS = 512
FFN_COLS = V7X_MXU_DIM
HG_CHUNK = 128
SAMPLE_GROUP = 8
SAMPLE_KEYS = WINDOW + DEC_SEQ
SAMPLE_KEYS_PAD = SAMPLE_KEYS + (-SAMPLE_KEYS) % V7X_SUBLANES

_F32 = jnp.float32
_BF16 = jnp.bfloat16
_NEG_INF = float("-inf")
_LOG2_E = 1.0 / math.log(2.0)


def _dot(a, b):
    return jnp.dot(a, b, preferred_element_type=_F32)


def _dot_nt(a, b):
    return lax.dot_general(a, b, (((1,), (1,)), ((), ())), preferred_element_type=_F32)


def _dot_tn(a, b):
    return lax.dot_general(a, b, (((0,), (0,)), ((), ())), preferred_element_type=_F32)


def _rms(x, g):
    return x * lax.rsqrt(jnp.mean(x * x, axis=-1, keepdims=True) + RMS_EPS) * g


def _t5_bucket_np(dist):
    dist = np.asarray(dist)
    n = np.maximum(dist, 0)
    max_exact = NUM_BUCKETS // 2
    nf = np.maximum(n, 1).astype(np.float64)
    large = max_exact + (np.log(nf / max_exact) / math.log(MAX_DISTANCE / max_exact)
                         * (NUM_BUCKETS - max_exact)).astype(np.int32)
    large = np.minimum(large, NUM_BUCKETS - 1)
    bucket = np.where(n < max_exact, n, large)
    return np.where((dist >= 0) & (dist < WINDOW), bucket, -1).astype(np.int32)


def _bias_from_buckets(bidx, relb_ref, head):
    def body(b, acc):
        return jnp.where(bidx == b, relb_ref[b * ATT_HEADS + head], acc)
    return lax.fori_loop(0, NUM_BUCKETS, body, jnp.full(bidx.shape, _NEG_INF, _F32))


def _time_major(ref):
    return jnp.concatenate([ref[:, t, :] for t in range(ref.shape[1])], axis=0)


def _proj_sample_kernel(x_ref, g_ref, w_ref, att_ref, hg_ref):
    h = _rms(_time_major(x_ref), g_ref[...]).astype(_BF16)
    res = _dot(h, w_ref[...])
    att_ref[...] = res[:, :ATT_SLAB]
    hg_ref[...] = res[:, ATT_SLAB:]


def _proj_sample(x, g, w):
    t = x.shape[0] * x.shape[1]

    def full(a):
        return pl.BlockSpec(a.shape, lambda i: (0,) * a.ndim, pipeline_mode=pl.Buffered(1))

    return pl.pallas_call(
        _proj_sample_kernel,
        grid=(1,),
        in_specs=[full(x), full(g), full(w)],
        out_specs=[
            pl.BlockSpec((t, ATT_SLAB), lambda i: (0, 0)),
            pl.BlockSpec((t, HG_SLAB), lambda i: (0, 0)),
        ],
        out_shape=[
            jax.ShapeDtypeStruct((t, ATT_SLAB), _F32),
            jax.ShapeDtypeStruct((t, HG_SLAB), _F32),
        ],
        compiler_params=pltpu.CompilerParams(
            dimension_semantics=("arbitrary",), vmem_limit_bytes=V7X_VMEM_LIMIT_BYTES),
        name="proj_sample",
    )(x, g, w)


def _proj_prompt_kernel(x_ref, g_ref, w_ref, wo_ref, wfi_ref, wfo_ref, att_ref, hg_ref, tail_ref,
                        wb_ref, wob_ref, wfib_ref, wfob_ref):
    i = pl.program_id(0)

    @pl.when(i == 0)
    def _():
        rows = D_MODEL // WEIGHT_CAST_STEPS
        for r in range(WEIGHT_CAST_STEPS):
            wb_ref[r * rows:(r + 1) * rows, :] = w_ref[r * rows:(r + 1) * rows, :].astype(_BF16)

    @pl.when(i < WEIGHT_CAST_STEPS)
    def _():
        wob_ref[...] = wo_ref[...].astype(_BF16)
        wfib_ref[...] = wfi_ref[...].astype(_BF16)
        wfob_ref[...] = wfo_ref[...].astype(_BF16)

    h = _rms(x_ref[...], g_ref[...]).astype(_BF16)
    res = _dot(h, wb_ref[...])
    q = res[:, :ATT_WIDTH] * (ATT_HEAD_DIM ** -0.5)
    att_ref[...] = jnp.concatenate([q, res[:, ATT_WIDTH:ATT_SLAB]], axis=-1).astype(_BF16)
    hg_ref[...] = res[:, ATT_SLAB:]
    tail_ref[...] = res[x_ref.shape[0] - WINDOW:, ATT_WIDTH:ATT_SLAB]


def _proj_prompt(x, g, w, w_o, wf_in, wf_out, *, rows_per_seq):
    t = x.shape[0]
    tm = PROJ_ROWS
    bps = rows_per_seq // tm
    nc = WEIGHT_CAST_STEPS
    assert t // tm >= nc

    def slab(a):
        return pl.BlockSpec((a.shape[0] // nc, a.shape[1]), lambda i: (jnp.minimum(i, nc - 1), 0))

    return pl.pallas_call(
        _proj_prompt_kernel,
        grid=(t // tm,),
        in_specs=[
            pl.BlockSpec((tm, D_MODEL), lambda i: (i, 0)),
            pl.BlockSpec((1, D_MODEL), lambda i: (0, 0)),
            pl.BlockSpec((D_MODEL, PROJ_WIDTH), lambda i: (0, 0), pipeline_mode=pl.Buffered(1)),
            slab(w_o), slab(wf_in), slab(wf_out),
        ],
        out_specs=[
            pl.BlockSpec((tm, ATT_SLAB), lambda i: (i, 0)),
            pl.BlockSpec((tm, HG_SLAB), lambda i: (i, 0)),
            pl.BlockSpec((None, WINDOW, 2 * KV_WIDTH), lambda i: (i // bps, 0, 0)),
            pl.BlockSpec((D_MODEL, PROJ_WIDTH), lambda i: (0, 0)),
            slab(w_o), slab(wf_in), slab(wf_out),
        ],
        out_shape=[
            jax.ShapeDtypeStruct((t, ATT_SLAB), _BF16),
            jax.ShapeDtypeStruct((t, HG_SLAB), _F32),
            jax.ShapeDtypeStruct((t // rows_per_seq, WINDOW, 2 * KV_WIDTH), _F32),
            jax.ShapeDtypeStruct(w.shape, _BF16),
            jax.ShapeDtypeStruct(w_o.shape, _BF16),
            jax.ShapeDtypeStruct(wf_in.shape, _BF16),
            jax.ShapeDtypeStruct(wf_out.shape, _BF16),
        ],
        compiler_params=pltpu.CompilerParams(
            dimension_semantics=("arbitrary",), vmem_limit_bytes=V7X_VMEM_LIMIT_BYTES),
        name="proj_prompt",
    )(x, g, w, w_o, wf_in, wf_out)


def _attn_sample_kernel(relb_ref, sink_ref, bidx_ref, q_ref, kvn_ref, kc_ref, vc_ref,
                        o_ref, nk_ref, nv_ref, bias_scr, sink_scr):
    @pl.when(pl.program_id(0) == 0)
    def _():
        bidx = bidx_ref[...]
        row_g = lax.broadcasted_iota(jnp.int32, (ATT_GROUP * DEC_SEQ, V7X_LANES), 0) // DEC_SEQ
        for kv in range(ATT_KV_HEADS):
            bias = jnp.full(bidx.shape, _NEG_INF, _F32)
            sk = jnp.zeros(row_g.shape, _F32)
            for g in range(ATT_GROUP):
                h = kv * ATT_GROUP + g
                rows = lax.broadcasted_iota(jnp.int32, bidx.shape, 0) // DEC_SEQ == g
                bias = jnp.where(rows, _bias_from_buckets(bidx, relb_ref, h), bias)
                sk = jnp.where(row_g == g, sink_ref[h], sk)
            bias_scr[kv] = bias
            sink_scr[kv] = sk

    keep = WINDOW - DEC_SEQ
    units = [(b, kv) for b in range(SAMPLE_GROUP) for kv in range(ATT_KV_HEADS)]
    scores = {}
    for b in range(SAMPLE_GROUP):
        kvn = kvn_ref[b]
        nk_ref[b, :keep, :] = kc_ref[b, DEC_SEQ:, :]
        nv_ref[b, :keep, :] = vc_ref[b, DEC_SEQ:, :]
        nk_ref[b, keep:, :] = kvn[:DEC_SEQ, :KV_WIDTH]
        nv_ref[b, keep:, :] = kvn[:DEC_SEQ, KV_WIDTH:]
        k = jnp.concatenate([kc_ref[b], kvn[:, :KV_WIDTH]], axis=0).astype(_BF16)
        for kv in range(ATT_KV_HEADS):
            q = (q_ref[b, kv] * (ATT_HEAD_DIM ** -0.5)).astype(_BF16)
            scores[b, kv] = _dot_nt(q, k)
    probs = {}
    for b, kv in units:
        s = scores[b, kv] + bias_scr[kv]
        sink = sink_scr[kv][:, :1]
        m = jnp.maximum(jnp.max(s, axis=-1, keepdims=True), sink)
        p = jnp.exp(s - m)
        den = jnp.sum(p, axis=-1, keepdims=True) + jnp.exp(sink - m)
        probs[b, kv] = (p.astype(_BF16), den)
    for b in range(SAMPLE_GROUP):
        v = jnp.concatenate([vc_ref[b], kvn_ref[b][:, KV_WIDTH:]], axis=0).astype(_BF16)
        for kv in range(ATT_KV_HEADS):
            p, den = probs[b, kv]
            o = _dot(p, v) / den
            o_ref[b, kv] = o[:, kv * ATT_HEAD_DIM:(kv + 1) * ATT_HEAD_DIM]


def _attn_sample(q_r, kv_new, cache_k, cache_v, rel_bias, sinks):
    db = q_r.shape[0]
    rows = ATT_GROUP * DEC_SEQ
    new_rows = SAMPLE_KEYS_PAD - WINDOW
    t = (np.arange(rows) % DEC_SEQ)[:, None]
    c = np.arange(SAMPLE_KEYS_PAD)[None, :]
    dist = np.where(c < SAMPLE_KEYS, t + WINDOW - c, -1)
    bidx = jnp.asarray(_t5_bucket_np(dist))
    smem = pl.BlockSpec(memory_space=pltpu.SMEM)
    g = SAMPLE_GROUP
    return pl.pallas_call(
        _attn_sample_kernel,
        grid=(db // g,),
        in_specs=[
            smem, smem,
            pl.BlockSpec((rows, SAMPLE_KEYS_PAD), lambda i: (0, 0)),
            pl.BlockSpec((g, ATT_KV_HEADS, rows, KV_WIDTH), lambda i: (i, 0, 0, 0)),
            pl.BlockSpec((g, new_rows, 2 * KV_WIDTH), lambda i: (i, 0, 0)),
            pl.BlockSpec((g, WINDOW, KV_WIDTH), lambda i: (i, 0, 0)),
            pl.BlockSpec((g, WINDOW, KV_WIDTH), lambda i: (i, 0, 0)),
        ],
        out_specs=[
            pl.BlockSpec((g, ATT_KV_HEADS, rows, ATT_HEAD_DIM), lambda i: (i, 0, 0, 0)),
            pl.BlockSpec((g, WINDOW, KV_WIDTH), lambda i: (i, 0, 0)),
            pl.BlockSpec((g, WINDOW, KV_WIDTH), lambda i: (i, 0, 0)),
        ],
        out_shape=[
            jax.ShapeDtypeStruct((db, ATT_KV_HEADS, rows, ATT_HEAD_DIM), _F32),
            jax.ShapeDtypeStruct((db, WINDOW, KV_WIDTH), _F32),
            jax.ShapeDtypeStruct((db, WINDOW, KV_WIDTH), _F32),
        ],
        scratch_shapes=[
            pltpu.VMEM((ATT_KV_HEADS, rows, SAMPLE_KEYS_PAD), _F32),
            pltpu.VMEM((ATT_KV_HEADS, rows, V7X_LANES), _F32),
        ],
        compiler_params=pltpu.CompilerParams(dimension_semantics=("arbitrary",)),
        name="attn_sample",
    )(rel_bias.reshape(-1), sinks, bidx, q_r, kv_new, cache_k, cache_v)


def _lower_bound(lbg):
    mx = jnp.max(lbg, axis=0, keepdims=True)
    e = jnp.exp(lbg - mx)
    return e[0:1] / jnp.sum(e, axis=0, keepdims=True)


def _hgrn_gates(hg, lb):
    q = jax.nn.silu(hg[:, 0:HG_WIDTH])
    f = lb + (1.0 - lb) * jax.nn.sigmoid(hg[:, HG_WIDTH:2 * HG_WIDTH])
    v = hg[:, 2 * HG_WIDTH:3 * HG_WIDTH]
    gate = jax.nn.silu(hg[:, 3 * HG_WIDTH:4 * HG_WIDTH])
    return q, jnp.log(f), 1.0 - f, v, gate


def _hgrn_levels(c):
    return [c >> (i + 1) for i in range(int(math.log2(c)))]


def _hgrn_small_levels(c):
    return [h for h in _hgrn_levels(c) if 2 * h < V7X_SUBLANES]


def _hgrn_tables(c):
    lvl = np.full((c, c), -1, np.int32)
    j = np.arange(c)[None, :]
    t = np.arange(c)[:, None]
    mats = [j <= t]
    for li, h in enumerate(_hgrn_levels(c)):
        mid = (t // (2 * h)) * (2 * h) + h
        same = (t // (2 * h)) == (j // (2 * h))
        lvl[same & (t >= mid) & (j < mid)] = li
        if h in _hgrn_small_levels(c):
            mats.append(np.where(t >= mid, (j >= mid) & (j <= t), (j > t) & (j < mid)))
    return np.concatenate(mats, axis=0).astype(np.float32), np.concatenate([lvl, lvl], axis=1)


def _pair_blockdiag(x):
    left = lax.broadcasted_iota(jnp.int32, x.shape, 1) < HG_KEY
    zero = jnp.zeros_like(x)
    return jnp.concatenate([jnp.where(left, x, zero), jnp.where(left, zero, x)], axis=0)


def _exact_rows_matmul_stacked(m3, x):
    h1 = x.astype(_BF16)
    r1 = x - h1.astype(_F32)
    h2 = r1.astype(_BF16)
    h3 = (r1 - h2.astype(_F32)).astype(_BF16)
    return _dot(m3, jnp.concatenate([h3, h2, h1], axis=0))


def _hgrn_level_rows(h, c):
    return [(slice(j * 2 * h, j * 2 * h + h), slice(j * 2 * h + h, (j + 1) * 2 * h))
            for j in range(c // (2 * h))]


def _hgrn_level_operands(q, k, sums, cb_ref, li, c):
    h = _hgrn_levels(c)[li]
    cb = sums[:c]
    small = _hgrn_small_levels(c)
    if h % V7X_SUBLANES == 0:
        lhs, rhs = [], []
        for j, (up, low) in enumerate(_hgrn_level_rows(h, c)):
            mid = cb_ref[pl.ds(j * 2 * h + h - 1, 1), :]
            lhs.append(q[low] * jnp.exp2(cb[low] - mid))
            rhs.append(k[up] * jnp.exp2(mid - cb[up]))
            rhs.append(jnp.zeros_like(k[low]))
        return (jnp.concatenate(lhs, axis=0).astype(_BF16),
                jnp.concatenate(rhs, axis=0).astype(_BF16), True)
    if h in small:
        i = 1 + small.index(h)
        arg = sums[i * c:(i + 1) * c]
    else:
        mid_cb = jnp.concatenate(
            [jnp.broadcast_to(cb_ref[pl.ds(j * 2 * h + h - 1, 1), :], (2 * h, cb.shape[1]))
             for j in range(c // (2 * h))], axis=0)
        arg = -jnp.abs(cb - mid_cb)
    e = jnp.exp2(arg)
    return (q * e).astype(_BF16), (k * e).astype(_BF16), False


def _hgrn_merge_level(a, pr, lvl, li, lower_only, c):
    if not lower_only:
        return jnp.where(lvl == li, pr, a)
    h = _hgrn_levels(c)[li]
    pieces = []
    for j, (up, low) in enumerate(_hgrn_level_rows(h, c)):
        pieces.append(a[up])
        pieces.append(jnp.where(lvl[low] == li, pr[j * h:(j + 1) * h], a[low]))
    return jnp.concatenate(pieces, axis=0)


def _hgrn_sample_kernel(hg_ref, lbg_ref, g_ref, mask_ref, s0_ref, o_ref, s_ref):
    nb = SAMPLE_GROUP
    lb = _lower_bound(lbg_ref[...])
    q, cb, k, v, gate = [], [], [], [], []
    for t in range(DEC_SEQ):
        qt, lf, kt, vt, gt = _hgrn_gates(hg_ref[t], lb)
        q.append(qt); k.append(kt); v.append(vt); gate.append(gt)
        cb.append(lf if t == 0 else cb[-1] + lf)
    total = cb[-1]
    mask = mask_ref[...]
    for h in range(HG_HEADS):
        hs = slice(h * HG_KEY, (h + 1) * HG_KEY)
        qdec = jnp.concatenate([q[t][:, hs] * jnp.exp(cb[t][:, hs]) for t in range(DEC_SEQ)], axis=0)
        kend = jnp.concatenate(
            [k[t][:, hs] * jnp.exp(total[:, hs] - cb[t][:, hs]) for t in range(DEC_SEQ)], axis=0)
        vall = jnp.concatenate([v[t][:, hs] for t in range(DEC_SEQ)], axis=0).astype(_BF16)
        qblk = (jnp.concatenate([qdec] * nb, axis=1) * mask).astype(_BF16)
        kblk = (jnp.concatenate([kend] * nb, axis=1) * mask).astype(_BF16)
        s0 = s0_ref[:, h]
        o_inter = _dot(qblk, s0.reshape(nb * HG_KEY, HG_VAL).astype(_BF16))
        upd = _dot_tn(kblk, vall)
        decay_t = jnp.exp(total[:, hs]).T
        for b in range(nb):
            s_ref[b, h] = s0[b] * decay_t[:, b:b + 1] + upd[b * HG_KEY:(b + 1) * HG_KEY]
        for t in range(DEC_SEQ):
            o = o_inter[t * nb:(t + 1) * nb]
            for s in range(t + 1):
                w = q[t][:, hs] * k[s][:, hs]
                if s < t:
                    w = w * jnp.exp(cb[t][:, hs] - cb[s][:, hs])
                o = o + jnp.sum(w, axis=-1, keepdims=True) * v[s][:, hs]
            o = o * lax.rsqrt(jnp.mean(o * o, axis=-1, keepdims=True) + RMS_EPS)
            o_ref[t, :, hs] = o * g_ref[:, hs] * gate[t][:, hs]


def _hgrn_sample(hg, s0, lb_gamma, hg_g):
    l, db, _ = hg.shape
    nb = SAMPLE_GROUP
    rb = np.arange(l * nb)[:, None] % nb
    cbk = np.arange(nb * HG_KEY)[None, :] // HG_KEY
    mask = jnp.asarray((rb == cbk).astype(np.float32))
    return pl.pallas_call(
        _hgrn_sample_kernel,
        grid=(db // nb,),
        in_specs=[
            pl.BlockSpec((l, nb, HG_SLAB), lambda i: (0, i, 0)),
            pl.BlockSpec(lb_gamma.shape, lambda i: (0, 0)),
            pl.BlockSpec((1, HG_WIDTH), lambda i: (0, 0)),
            pl.BlockSpec(mask.shape, lambda i: (0, 0)),
            pl.BlockSpec((nb, HG_HEADS, HG_KEY, HG_VAL), lambda i: (i, 0, 0, 0)),
        ],
        out_specs=[
            pl.BlockSpec((l, nb, HG_WIDTH), lambda i: (0, i, 0)),
            pl.BlockSpec((nb, HG_HEADS, HG_KEY, HG_VAL), lambda i: (i, 0, 0, 0)),
        ],
        out_shape=[
            jax.ShapeDtypeStruct((l, db, HG_WIDTH), _F32),
            jax.ShapeDtypeStruct((db, HG_HEADS, HG_KEY, HG_VAL), _F32),
        ],
        compiler_params=pltpu.CompilerParams(
            dimension_semantics=("arbitrary",), vmem_limit_bytes=V7X_VMEM_LIMIT_BYTES),
        name="hgrn_sample",
    )(hg, lb_gamma, hg_g, mask, s0)


def _col_chunks(n, wide):
    chunks = [(lo, wide) for lo in range(0, n - n % wide, wide)]
    if n % wide:
        chunks.append((n - n % wide, n % wide))
    return chunks


def _ffn_col_chunks(n):
    return _col_chunks(n, FFN_COLS)


def _ffn_sample_kernel(x_ref, oa_ref, oh_ref, ag_ref, wo_ref, n2_ref, wfi_ref, cw_ref, cb_ref, wfo_ref,
                       fg_ref, prev_ref, y_ref, conv_ref, buf, hid):
    shift = x_ref.shape[0]
    tm = shift * x_ref.shape[1]
    pad = buf.shape[0] - tm
    oa = _rms(oa_ref[...], ag_ref[...])
    mix = jnp.concatenate([oa.astype(_BF16), oh_ref[...].astype(_BF16)], axis=-1)
    x1 = _time_major(x_ref) + _dot(mix, wo_ref[...])
    h2 = _rms(x1, n2_ref[...]).astype(_BF16)
    buf[:pad, :] = _time_major(prev_ref)
    for lo, width in _ffn_col_chunks(D_FF):
        cs = slice(lo, lo + width)
        a = _dot(h2, wfi_ref[:, cs])
        gate = _dot(h2, wfi_ref[:, D_FF + lo:D_FF + lo + width])
        buf[pad:, cs] = a
        ac = cb_ref[:, cs] + buf[pad - 2 * shift:pad - 2 * shift + tm, cs] * cw_ref[0:1, cs]
        ac = ac + buf[pad - shift:pad - shift + tm, cs] * cw_ref[1:2, cs]
        ac = ac + a * cw_ref[2:3, cs]
        hid[:, cs] = (jax.nn.silu(ac) * gate).astype(_BF16)
    y = _rms(x1 + _dot(hid[...], wfo_ref[...]), fg_ref[...])
    for t in range(y_ref.shape[1]):
        y_ref[:, t, :] = y[t * shift:(t + 1) * shift]
    for j in range(conv_ref.shape[1]):
        conv_ref[:, j, :] = buf[tm + j * shift:tm + (j + 1) * shift, :]


def _ffn_sample(x, oa, oh, attn_g, w_o, n2, wf_in, conv_w, conv_b, wf_out, final_g, prev):
    t = x.shape[0] * x.shape[1]
    pad = prev.shape[0] * prev.shape[1]

    def full(a):
        return pl.BlockSpec(a.shape, lambda i: (0,) * a.ndim, pipeline_mode=pl.Buffered(1))

    args = (x, oa, oh, attn_g, w_o, n2, wf_in, conv_w, conv_b, wf_out, final_g, prev)
    return pl.pallas_call(
        _ffn_sample_kernel,
        grid=(1,),
        in_specs=[full(a) for a in args],
        out_specs=[pl.BlockSpec(x.shape, lambda i: (0, 0, 0)),
                   pl.BlockSpec(prev.shape, lambda i: (0, 0, 0))],
        out_shape=[jax.ShapeDtypeStruct(x.shape, _F32),
                   jax.ShapeDtypeStruct(prev.shape, _F32)],
        scratch_shapes=[pltpu.VMEM((pad + t, D_FF), _F32), pltpu.VMEM((t, D_FF), _BF16)],
        compiler_params=pltpu.CompilerParams(
            dimension_semantics=("arbitrary",), vmem_limit_bytes=V7X_VMEM_LIMIT_BYTES),
        name="ffn_sample",
    )(*args)


def _interleave(streams):
    done = [0] * len(streams)
    live = set(range(len(streams)))
    while live:
        idx = min(live, key=lambda s: (done[s] + 1) / streams[s][1])
        try:
            next(streams[idx][0])
            done[idx] += 1
        except StopIteration:
            live.discard(idx)


def _attn_stream(q_ref, kvp_ref, kvc_ref, bias_scr, ag_ref, out_ref, seq_start):
    w = WINDOW
    pair = 2 * ATT_HEAD_DIM
    half_g = ATT_GROUP // 2
    rows = q_ref.shape[0]
    kvx = jnp.concatenate([kvp_ref[...], kvc_ref[...]], axis=0)
    kx = kvx[:, :pair]
    kr = pltpu.roll(kx.astype(_F32), ATT_HEAD_DIM, 1).astype(_BF16)
    lo = lax.broadcasted_iota(jnp.int32, kx.shape, 1) < ATT_HEAD_DIM
    zero = jnp.zeros_like(kx)
    kmat = ((jnp.where(lo, kx, zero), jnp.where(lo, kr, zero)),
            (jnp.where(lo, zero, kr), jnp.where(lo, zero, kx)))
    v_t = kvx[:, pair:].astype(_F32).T.astype(_BF16)
    row = lax.broadcasted_iota(jnp.int32, (2 * w, half_g * w), 0)
    first_key = lax.broadcasted_iota(jnp.int32, (2 * w, pair), 0) == 0
    first_col = lax.broadcasted_iota(jnp.int32, (ATT_HEAD_DIM, 2 * w), 1) == 0
    no_prev = (row >= 1) & (row < w) & seq_start
    units = [(kv, par) for kv in range(ATT_KV_HEADS) for par in range(2)]
    yield
    for j in range(rows // w):
        qs = slice(j * w, (j + 1) * w)
        ks = slice(j * w, (j + 2) * w)
        scores = {}
        for kv in range(ATT_KV_HEADS):
            base = kv * ATT_GROUP * ATT_HEAD_DIM
            lhs = jnp.concatenate(
                [q_ref[qs, base + i * pair:base + (i + 1) * pair] for i in range(half_g)], axis=0)
            for par in range(2):
                keys = jnp.where(first_key, jnp.zeros((), _BF16), kmat[par][kv][ks])
                scores[kv, par] = _dot_nt(keys, lhs)
        yield
        probs = {}
        for kv, par in units:
            s = scores[kv, par] + bias_scr[kv, par]
            if j == 0:
                s = jnp.where(no_prev, _NEG_INF, s)
            m = jnp.max(s, axis=0, keepdims=True)
            p = jnp.exp(s - m)
            probs[kv, par] = (p.astype(_BF16), jnp.sum(p, axis=0, keepdims=True))
        yield
        outs = {}
        for kv, par in units:
            p, den = probs[kv, par]
            vals = v_t[kv * ATT_HEAD_DIM:(kv + 1) * ATT_HEAD_DIM, ks]
            vals = jnp.where(first_col, jnp.zeros((), _BF16), vals)
            outs[kv, par] = _dot(vals, p) / den
        pieces = []
        for kv in range(ATT_KV_HEADS):
            for i in range(half_g):
                both = jnp.concatenate(
                    [outs[kv, par][:, i * w:(i + 1) * w] for par in range(2)], axis=0)
                pieces.append(both.T)
        o = jnp.concatenate(pieces, axis=-1)
        out_ref[qs, :ATT_WIDTH] = _rms(o, ag_ref[...]).astype(_BF16)
        yield


def _hgrn_stream(hg_ref, lbg_ref, g_ref, m_ref, lvl_ref, st_scr, cb_scr, out_ref, seq_start):
    c = HG_CHUNK
    pw = 2 * HG_KEY
    pairs = range(HG_HEADS // 2)
    lb = _lower_bound(lbg_ref[...])
    lvl = lvl_ref[...]
    rr = lax.broadcasted_iota(jnp.int32, (pw, pw), 0) < HG_VAL
    cc = lax.broadcasted_iota(jnp.int32, (pw, pw), 1) < HG_KEY
    same_head = rr == cc
    for ci in range(hg_ref.shape[0] // c):
        rs = slice(ci * c, (ci + 1) * c)
        q, k, v, gate, sums = {}, {}, {}, {}, {}
        for p in pairs:
            ps = slice(p * pw, (p + 1) * pw)
            q[p] = jax.nn.silu(hg_ref[rs, p * pw:(p + 1) * pw])
            f = lb[:, ps] + (1.0 - lb[:, ps]) * jax.nn.sigmoid(
                hg_ref[rs, HG_WIDTH + p * pw:HG_WIDTH + (p + 1) * pw])
            v[p] = hg_ref[rs, 2 * HG_WIDTH + p * pw:2 * HG_WIDTH + (p + 1) * pw]
            gate[p] = jax.nn.silu(hg_ref[rs, 3 * HG_WIDTH + p * pw:3 * HG_WIDTH + (p + 1) * pw])
            k[p] = 1.0 - f
            sums[p] = _exact_rows_matmul_stacked(m_ref[...], jnp.log(f) * _LOG2_E)
        yield
        for p in pairs:
            cb_scr[p] = sums[p][:c]
        levels = _hgrn_levels(c)
        a = {p: jnp.zeros((c, pw), _F32) for p in pairs}
        ops = {p: _hgrn_level_operands(q[p], k[p], sums[p], cb_scr.at[p], 0, c) for p in pairs}
        yield
        for li in range(len(levels)):
            nxt = None
            if li + 1 < len(levels):
                nxt = {p: _hgrn_level_operands(q[p], k[p], sums[p], cb_scr.at[p], li + 1, c)
                       for p in pairs}
            for p in pairs:
                lhs, rhs, low = ops[p]
                pr = _dot_nt(lhs, _pair_blockdiag(rhs))
                a[p] = _hgrn_merge_level(a[p], pr, lvl, li, low, c)
            ops = nxt
            yield
        for p in pairs:
            cb = sums[p][:c]
            total = cb[c - 1:c, :]
            vb = v[p].astype(_BF16)
            st = st_scr[p]
            if ci == 0:
                st = jnp.where(seq_start, 0.0, st)
            o = (_dot_nt((q[p] * jnp.exp2(cb)).astype(_BF16), st.astype(_BF16))
                 + _dot(a[p].astype(_BF16), _pair_blockdiag(vb)))
            upd = _dot_tn(vb, (k[p] * jnp.exp2(total - cb)).astype(_BF16))
            st_scr[p] = st * jnp.exp2(total) + jnp.where(same_head, upd, 0.0)
            qk = q[p] * k[p]
            for hh in range(2):
                hs = slice(hh * HG_KEY, (hh + 1) * HG_KEY)
                gs = slice(p * pw + hh * HG_KEY, p * pw + (hh + 1) * HG_KEY)
                oh = o[:, hs] + jnp.sum(qk[:, hs], axis=-1, keepdims=True) * v[p][:, hs]
                oh = oh * lax.rsqrt(jnp.mean(oh * oh, axis=-1, keepdims=True) + RMS_EPS)
                out_ref[rs, ATT_WIDTH + gs.start:ATT_WIDTH + gs.stop] = (
                    oh * g_ref[:, gs] * gate[p][:, hs]).astype(_BF16)
        yield


def _ffn_stream(mix_ref, x_ref, wo_ref, n2_ref, wfi_ref, cw_ref, cb_ref, wfo_ref, fg_ref,
                y_ref, conv_ref, cbuf, carry, hid, seq_start):
    tm = x_ref.shape[0]
    pad = carry.shape[0]
    x1 = x_ref[...] + _dot(mix_ref[...], wo_ref[...])
    y_ref[...] = x1
    h2 = _rms(x1, n2_ref[...]).astype(_BF16)
    yield
    for c, (lo, width) in enumerate(_ffn_col_chunks(D_FF)):
        cs = slice(lo, lo + width)
        buf = cbuf.at[c % 2]
        a = _dot(h2, wfi_ref[:, cs])
        gate = _dot(h2, wfi_ref[:, D_FF + lo:D_FF + lo + width])
        buf[:pad, :width] = jnp.where(seq_start, 0.0, carry[:, cs])
        buf[pad:, :width] = a
        carry[:, cs] = a[tm - pad:, :]
        ac = cb_ref[:, cs] + buf[pad - 2:pad - 2 + tm, :width] * cw_ref[0:1, cs]
        ac = ac + buf[pad - 1:pad - 1 + tm, :width] * cw_ref[1:2, cs]
        ac = ac + a * cw_ref[2:3, cs]
        hid[:, cs] = (jax.nn.silu(ac) * gate).astype(_BF16)
        yield
    conv_ref[...] = carry[pad - (CONV_W - 1):, :]
    for lo, width in _ffn_col_chunks(D_MODEL):
        ns = slice(lo, lo + width)
        y_ref[:, ns] = y_ref[:, ns] + _dot(hid[...], wfo_ref[:, ns])
        yield
    y_ref[...] = _rms(y_ref[...], fg_ref[...])


def _layer_kernel(relb_ref, sink_ref, bidx_ref, q_ref, kvp_ref, kvc_ref, hg_ref, lbg_ref, hgg_ref,
                  m_ref, lvl_ref, x_ref, ag_ref, wo_ref, n2_ref, wfi_ref, cw_ref, cb_ref, wfo_ref,
                  fg_ref, y_ref, conv_ref, s_ref, bias_scr, st_scr, cb_scr, mix_scr, cbuf, carry, hid,
                  *, blocks_per_seq):
    i = pl.program_id(0)
    nblk = pl.num_programs(0) - 1
    w = WINDOW

    @pl.when(i == 0)
    def _():
        bidx = bidx_ref[...]
        key0 = lax.broadcasted_iota(jnp.int32, bidx.shape, 0) == 0
        for kv in range(ATT_KV_HEADS):
            for par in range(2):
                for g in range(ATT_GROUP // 2):
                    h = kv * ATT_GROUP + 2 * g + par
                    tab = _bias_from_buckets(bidx, relb_ref, h)
                    bias_scr[kv, par, :, g * w:(g + 1) * w] = jnp.where(key0, sink_ref[h], tab)
        st_scr[...] = jnp.zeros(st_scr.shape, _F32)
        mix_scr[...] = jnp.zeros(mix_scr.shape, _BF16)
        carry[...] = jnp.zeros(carry.shape, _F32)

    slot = i % 2
    mix_start = (jnp.minimum(i, nblk - 1) % blocks_per_seq) == 0
    ffn_start = (jnp.maximum(i - 1, 0) % blocks_per_seq) == 0
    rows = q_ref.shape[0]
    chunks = rows // HG_CHUNK
    _interleave([
        (_ffn_stream(mix_scr.at[1 - slot], x_ref, wo_ref, n2_ref, wfi_ref, cw_ref, cb_ref, wfo_ref,
                     fg_ref, y_ref, conv_ref, cbuf, carry, hid, ffn_start),
         1 + len(_ffn_col_chunks(D_FF)) + len(_ffn_col_chunks(D_MODEL)) + 1),
        (_hgrn_stream(hg_ref, lbg_ref, hgg_ref, m_ref, lvl_ref, st_scr, cb_scr, mix_scr.at[slot],
                      mix_start), chunks * (3 + len(_hgrn_levels(HG_CHUNK)))),
        (_attn_stream(q_ref, kvp_ref, kvc_ref, bias_scr, ag_ref, mix_scr.at[slot], mix_start),
         1 + 3 * (rows // w)),
    ])

    @pl.when((i % blocks_per_seq == blocks_per_seq - 1) & (i < nblk))
    def _():
        for p in range(HG_HEADS // 2):
            st = st_scr[p]
            s_ref[2 * p] = st[:HG_VAL, :HG_KEY].T
            s_ref[2 * p + 1] = st[HG_VAL:, HG_KEY:].T


def _layer(x, att, hg, rel_bias, sinks, lb_gamma, hg_g, attn_g, w_o, n2, wf_in, conv_w, conv_b,
           wf_out, final_g, *, rows_per_seq):
    t = x.shape[0]
    rows = LAYER_ROWS
    nblk = t // rows
    bps = rows_per_seq // rows
    nseq = nblk // bps
    pad = V7X_SUBLANES
    qi = np.arange(WINDOW)[None, :]
    si = np.arange(2 * WINDOW)[:, None]
    bidx = jnp.asarray(_t5_bucket_np(WINDOW + qi - si))
    mats, lvl = _hgrn_tables(HG_CHUNK)
    mats = np.tile(mats, (1, 3))
    kvcol = ATT_WIDTH // (2 * KV_WIDTH)
    wblocks = rows // WINDOW
    last = nblk - 1
    smem = pl.BlockSpec(memory_space=pltpu.SMEM)
    const = functools.partial(pl.BlockSpec, pipeline_mode=pl.Buffered(1))

    def mix_blk(i):
        return jnp.minimum(i, last)

    def ffn_blk(i):
        return jnp.maximum(i - 1, 0)

    in_specs = [
        smem, smem,
        const(bidx.shape, lambda i: (0, 0)),
        pl.BlockSpec((rows, ATT_WIDTH), lambda i: (mix_blk(i), 0)),
        pl.BlockSpec((WINDOW, 2 * KV_WIDTH),
                     lambda i: (jnp.maximum(mix_blk(i) * wblocks - 1, 0), kvcol)),
        pl.BlockSpec((rows, 2 * KV_WIDTH), lambda i: (mix_blk(i), kvcol)),
        pl.BlockSpec((rows, HG_SLAB), lambda i: (mix_blk(i), 0)),
        const(lb_gamma.shape, lambda i: (0, 0)),
        const((1, HG_WIDTH), lambda i: (0, 0)),
        const(mats.shape, lambda i: (0, 0)),
        const(lvl.shape, lambda i: (0, 0)),
        pl.BlockSpec((rows, D_MODEL), lambda i: (ffn_blk(i), 0)),
        const((1, ATT_WIDTH), lambda i: (0, 0)),
        const((ATT_WIDTH + HG_WIDTH, D_MODEL), lambda i: (0, 0)),
        const((1, D_MODEL), lambda i: (0, 0)),
        const((D_MODEL, 2 * D_FF), lambda i: (0, 0)),
        const((CONV_W, D_FF), lambda i: (0, 0)),
        const((1, D_FF), lambda i: (0, 0)),
        const((D_FF, D_MODEL), lambda i: (0, 0)),
        const((1, D_MODEL), lambda i: (0, 0)),
    ]
    return pl.pallas_call(
        functools.partial(_layer_kernel, blocks_per_seq=bps),
        grid=(nblk + 1,),
        in_specs=in_specs,
        out_specs=[
            pl.BlockSpec((rows, D_MODEL), lambda i: (ffn_blk(i), 0)),
            pl.BlockSpec((None, CONV_W - 1, D_FF), lambda i: (ffn_blk(i) // bps, 0, 0)),
            pl.BlockSpec((None, HG_HEADS, HG_KEY, HG_VAL), lambda i: (mix_blk(i) // bps, 0, 0, 0)),
        ],
        out_shape=[
            jax.ShapeDtypeStruct((t, D_MODEL), _F32),
            jax.ShapeDtypeStruct((nseq, CONV_W - 1, D_FF), _F32),
            jax.ShapeDtypeStruct((nseq, HG_HEADS, HG_KEY, HG_VAL), _F32),
        ],
        scratch_shapes=[
            pltpu.VMEM((ATT_KV_HEADS, 2, 2 * WINDOW, ATT_GROUP // 2 * WINDOW), _F32),
            pltpu.VMEM((HG_HEADS // 2, 2 * HG_VAL, 2 * HG_KEY), _F32),
            pltpu.VMEM((HG_HEADS // 2, HG_CHUNK, 2 * HG_KEY), _F32),
            pltpu.VMEM((2, rows, ATT_WIDTH + HG_WIDTH), _BF16),
            pltpu.VMEM((2, pad + rows, FFN_COLS), _F32),
            pltpu.VMEM((pad, D_FF), _F32),
            pltpu.VMEM((rows, D_FF), _BF16),
        ],
        compiler_params=pltpu.CompilerParams(
            dimension_semantics=("arbitrary",), vmem_limit_bytes=V7X_VMEM_LIMIT_BYTES),
        name="layer",
    )(rel_bias.reshape(-1), sinks, bidx, att, att, att, hg, lb_gamma, hg_g, jnp.asarray(mats, _BF16),
      jnp.asarray(lvl), x, attn_g, w_o, n2, wf_in, conv_w, conv_b, wf_out, final_g)


def kernel(x_prompt, x_sample, cache_k_win, cache_v_win, state_hgrn, state_conv, norm1_g, w_in,
           attn_sinks, rel_bias, lb_gamma, attn_out_g, hg_out_g, w_out, norm2_g, w_ffn_in, conv_w,
           conv_b, w_ffn_out, final_g):
    assert DEPTH == 1
    n1 = norm1_g[0][None]
    n2 = norm2_g[0][None]
    ag = attn_out_g[0][None]
    hgg = hg_out_g[0][None]
    fg = final_g[None]
    cb = conv_b[0][None]
    cw = conv_w[0]
    sinks = attn_sinks[0]

    xp = x_prompt.reshape(BATCH * SEQ, D_MODEL)
    att_p, hg_p, kv_tail, w_in_b, w_o_b, wf_in_b, wf_out_b = _proj_prompt(
        xp, n1, w_in[0], w_out[0], w_ffn_in[0], w_ffn_out[0], rows_per_seq=SEQ)
    ffn_w = (ag, w_o_b, n2, wf_in_b, cw, cb, wf_out_b, fg)
    y_p, conv_p, s_p = _layer(xp, att_p, hg_p, rel_bias, sinks, lb_gamma, hgg, *ffn_w,
                              rows_per_seq=SEQ)
    new_k_p = kv_tail[:, :, :KV_WIDTH].reshape(1, BATCH, WINDOW, ATT_KV_HEADS, ATT_HEAD_DIM)
    new_v_p = kv_tail[:, :, KV_WIDTH:].reshape(1, BATCH, WINDOW, ATT_KV_HEADS, ATT_HEAD_DIM)

    rows_s = DEC_SEQ * DEC_BATCH
    att_s, hg_s = _proj_sample(x_sample, n1, w_in_b)
    att_s = att_s.reshape(DEC_SEQ, DEC_BATCH, ATT_SLAB)
    q_r = att_s[:, :, :ATT_WIDTH].reshape(DEC_SEQ, DEC_BATCH, ATT_KV_HEADS, ATT_GROUP, ATT_HEAD_DIM)
    q_r = q_r.transpose(1, 2, 3, 0, 4).reshape(DEC_BATCH, ATT_KV_HEADS, ATT_GROUP * DEC_SEQ, ATT_HEAD_DIM)
    q_zero = jnp.zeros_like(q_r[:, 0])
    q_r = jnp.stack([jnp.concatenate([q_r[:, 0], q_zero], axis=-1),
                     jnp.concatenate([q_zero, q_r[:, 1]], axis=-1)], axis=1)
    kv_new = att_s[:, :, ATT_WIDTH:].transpose(1, 0, 2)
    kv_new = jnp.pad(kv_new, ((0, 0), (0, SAMPLE_KEYS_PAD - SAMPLE_KEYS), (0, 0)))
    o16, new_k_s, new_v_s = _attn_sample(
        q_r, kv_new, cache_k_win[0].reshape(DEC_BATCH, WINDOW, KV_WIDTH),
        cache_v_win[0].reshape(DEC_BATCH, WINDOW, KV_WIDTH), rel_bias, sinks)
    oa_s = o16.reshape(DEC_BATCH, ATT_KV_HEADS, ATT_GROUP, DEC_SEQ, ATT_HEAD_DIM)
    oa_s = oa_s.transpose(3, 0, 1, 2, 4).reshape(rows_s, ATT_WIDTH)
    oh_s, s_s = _hgrn_sample(hg_s.reshape(DEC_SEQ, DEC_BATCH, HG_SLAB), state_hgrn[0], lb_gamma, hgg)
    y_s, conv_s = _ffn_sample(x_sample, oa_s, oh_s.reshape(rows_s, HG_WIDTH), *ffn_w, state_conv[0])

    cache_shape = (1, DEC_BATCH, WINDOW, ATT_KV_HEADS, ATT_HEAD_DIM)
    return (y_p.reshape(BATCH, SEQ, D_MODEL), y_s, new_k_p, new_v_p, s_p[None], conv_p[None],
            new_k_s.reshape(cache_shape), new_v_s.reshape(cache_shape), s_s[None], conv_s[None])
```
